```python
import math
import jax, jax.numpy as jnp
from jax import lax
import numpy as np

D_MODEL = 1024
BATCH = 8
SEQ = 4096
DEPTH = 2

GRID_W = 64
CTX_LEN = 256
N_MOD = 9
D_FF = 2816
ALPHA = (2 * DEPTH) ** 0.25
BETA = (8 * DEPTH) ** -0.25
EPS = 1e-6
A_HEADS = 6
A_KV_HEADS = 2
HEAD_DIM = 64
ROPE_THETA = 10000.0
Q_BLOCK = 128
HY_WIDTH = 256
HY_EMB = 33
HY_BANDS = (HY_EMB - 1) // 2
HY_ORDER = 64
HY_TARGET = 1e-2
HY_FAST = 0.3
HY_SLOW = 1.5
C_HEADS = 6
C_Q_LORA = 256
C_KV_LORA = 128
C_NOPE = 64
C_ROPE = 32
C_V = 64
A_WIDTH = A_HEADS * HEAD_DIM
C_WIDTH = C_HEADS * C_V
D_MIX = A_WIDTH + HY_WIDTH + C_WIDTH
A_COLS = (A_HEADS + 2 * A_KV_HEADS) * HEAD_DIM
B_COLS = 3 * HY_WIDTH
C_COLS = C_Q_LORA + C_KV_LORA + C_ROPE
P_IN = A_COLS + B_COLS + C_COLS

kernel_name = "hybrid_parallel_gqa_hyena_mla_block"


def layer_norm(x, g=None, b=None):
    xf = x.astype(jnp.float32)
    mu = jnp.mean(xf, -1, keepdims=True)
    var = jnp.mean(jnp.square(xf - mu), -1, keepdims=True)
    y = (xf - mu) * lax.rsqrt(var + EPS)
    if g is not None:
        y = y * g + b
    return y.astype(x.dtype)


def rms_norm(x, g):
    xf = x.astype(jnp.float32)
    y = xf * lax.rsqrt(jnp.mean(jnp.square(xf), -1, keepdims=True) + EPS) * g
    return y.astype(x.dtype)


def axial_rope(rows, rot_dim):
    row = jnp.repeat(jnp.arange(rows, dtype=jnp.float32), GRID_W)
    col = jnp.tile(jnp.arange(GRID_W, dtype=jnp.float32), rows)
    n_freq = rot_dim // 4
    inv = ROPE_THETA ** (-jnp.arange(n_freq, dtype=jnp.float32) / n_freq)
    ang = jnp.concatenate([row[:, None] * inv, col[:, None] * inv], -1)
    return jnp.cos(ang)[:, None, :], jnp.sin(ang)[:, None, :]


def apply_rope(x, cos, sin):
    xf = x.astype(jnp.float32)
    half = xf.shape[-1] // 2
    x1, x2 = xf[..., :half], xf[..., half:]
    return jnp.concatenate([x1 * cos - x2 * sin, x2 * cos + x1 * sin], -1).astype(x.dtype)


def block_attention(q, k, v, scale):
    b, n_q, n_heads, d = q.shape
    n_kv = k.shape[2]
    n_blk = n_q // Q_BLOCK
    qb = (q * scale).reshape(b, n_blk, Q_BLOCK, n_kv, n_heads // n_kv, d).transpose(1, 0, 2, 3, 4, 5)

    def one_block(q_blk):
        s = jnp.einsum('bqkgd,bskd->bkgqs', q_blk, k, preferred_element_type=jnp.float32)
        p = jax.nn.softmax(s, axis=-1).astype(v.dtype)
        return jnp.einsum('bkgqs,bske->bqkge', p, v)

    o = lax.map(one_block, qb)
    return o.transpose(1, 0, 2, 3, 4, 5).reshape(b, n_q, n_heads * v.shape[-1])


def gqa_qkv(pa, q_g, k_g, rope):
    b, n, _ = pa.shape
    q, k, v = jnp.split(pa, [A_WIDTH, A_WIDTH + A_KV_HEADS * HEAD_DIM], -1)
    q = rms_norm(q.reshape(b, n, A_HEADS, HEAD_DIM), q_g)
    k = rms_norm(k.reshape(b, n, A_KV_HEADS, HEAD_DIM), k_g)
    v = v.reshape(b, n, A_KV_HEADS, HEAD_DIM)
    if rope is not None:
        q = apply_rope(q, *rope)
        k = apply_rope(k, *rope)
    return q, k, v


def mla_qkv(pc, q_g, kv_g, w_uq, w_ukv, rope):
    b, n, _ = pc.shape
    c_q, c_kv, k_r = jnp.split(pc, [C_Q_LORA, C_Q_LORA + C_KV_LORA], -1)
    q = (rms_norm(c_q, q_g) @ w_uq).reshape(b, n, C_HEADS, C_NOPE + C_ROPE)
    kv = (rms_norm(c_kv, kv_g) @ w_ukv).reshape(b, n, C_HEADS, C_NOPE + C_V)
    q_nope, q_rope = q[..., :C_NOPE], q[..., C_NOPE:]
    k_nope, v = kv[..., :C_NOPE], kv[..., C_NOPE:]
    k_r = k_r[:, :, None, :]
    if rope is not None:
        q_rope = apply_rope(q_rope, *rope)
        k_r = apply_rope(k_r, *rope)
    q = jnp.concatenate([q_nope, q_rope], -1)
    k = jnp.concatenate([k_nope, jnp.broadcast_to(k_r, (b, n, C_HEADS, C_ROPE))], -1)
    return q, k, v


def hyena_filter(n, w1, b1, w2, b2, w3, b3, w4, freq):
    t = jnp.linspace(0.0, 1.0, n, dtype=jnp.float32)[:, None]
    w = 2.0 * math.pi * jnp.arange(n, dtype=jnp.float32)[:, None] / n
    f = jnp.linspace(1e-4, HY_BANDS - 1, HY_BANDS, dtype=jnp.float32)[None, :]
    z = jnp.concatenate([t, jnp.cos(f * w), -jnp.sin(f * w)], -1)
    hdn = jnp.sin(freq * (z @ w1 + b1))
    hdn = jnp.sin(freq * (hdn @ w2 + b2))
    hdn = jnp.sin(freq * (hdn @ w3 + b3))
    h = (hdn @ w4).astype(jnp.float32)
    deltas = jnp.abs(jnp.linspace(math.log(HY_TARGET) / HY_SLOW, math.log(HY_TARGET) / HY_FAST,
                                  HY_WIDTH, dtype=jnp.float32))
    h = h * jnp.exp(-t * jnp.tile(deltas, 2))
    h_fwd, h_bwd = h[:, :HY_WIDTH], h[:, HY_WIDTH:]
    filt = jnp.concatenate([h_fwd, jnp.zeros((1, HY_WIDTH), jnp.float32), h_bwd[:0:-1]], 0)
    return filt / jnp.sum(jnp.abs(filt), 0, keepdims=True)


def hyena_mix(p, conv_w, conv_b, w1, b1, w2, b2, w3, b3, w4, freq, d_skip):
    n = p.shape[1]
    pp = jnp.pad(p, ((0, 0), (1, 1), (0, 0)))
    p = pp[:, :-2] * conv_w[0] + pp[:, 1:-1] * conv_w[1] + pp[:, 2:] * conv_w[2] + conv_b
    v, x1, x0 = jnp.split(p, 3, -1)
    u = (v * x1).astype(jnp.float32)
    filt = hyena_filter(n, w1, b1, w2, b2, w3, b3, w4, freq)
    uf = jnp.fft.rfft(u, n=2 * n, axis=1)
    y = jnp.fft.irfft(uf * jnp.fft.rfft(filt, axis=0)[None], n=2 * n, axis=1)[:, :n]
    y = y + u * d_skip
    return (y * x0).astype(x0.dtype)


def modulate(x, shift, scale):
    return layer_norm(x) * (1.0 + scale) + shift


def swiglu(h, w_gu, w_down):
    g, u = jnp.split(h @ w_gu, 2, -1)
    return (jax.nn.silu(g) * u) @ w_down


def ffn_sublayer(x, m, w_gu, w_down, g, b):
    y = swiglu(modulate(x, m[:, 0], m[:, 1]), w_gu, w_down)
    return layer_norm(ALPHA * x + 0.5 * m[:, 2] * y, g, b)


def token_mix(h_lat, h_ctx, w_in, w_out, a_qn, a_kn, hy, mla, rope_a, rope_c, need_ctx):
    p_lat = h_lat @ w_in
    p_ctx = h_ctx @ w_in
    pa_l, pb_l, pc_l = jnp.split(p_lat, [A_COLS, A_COLS + B_COLS], -1)
    pa_c, pb_c, pc_c = jnp.split(p_ctx, [A_COLS, A_COLS + B_COLS], -1)
    qa_l, ka_l, va_l = gqa_qkv(pa_l, a_qn, a_kn, rope_a)
    qa_c, ka_c, va_c = gqa_qkv(pa_c, a_qn, a_kn, None)
    oa_l = block_attention(qa_l, jnp.concatenate([ka_c, ka_l], 1), jnp.concatenate([va_c, va_l], 1),
                           HEAD_DIM ** -0.5)
    qc_l, kc_l, vc_l = mla_qkv(pc_l, *mla, rope_c)
    qc_c, kc_c, vc_c = mla_qkv(pc_c, *mla, None)
    oc_l = block_attention(qc_l, jnp.concatenate([kc_c, kc_l], 1), jnp.concatenate([vc_c, vc_l], 1),
                           (C_NOPE + C_ROPE) ** -0.5)
    ob_l = hyena_mix(pb_l, *hy)
    out_lat = jnp.concatenate([oa_l, ob_l, oc_l], -1) @ w_out
    if not need_ctx:
        return out_lat, None
    oa_c = block_attention(qa_c, ka_c, va_c, HEAD_DIM ** -0.5)
    oc_c = block_attention(qc_c, kc_c, vc_c, (C_NOPE + C_ROPE) ** -0.5)
    ob_c = hyena_mix(pb_c, *hy)
    out_ctx = jnp.concatenate([oa_c, ob_c, oc_c], -1) @ w_out
    return out_lat, out_ctx


def setup_inputs(seed: int = 0) -> dict:
    key = jax.random.key(seed)
    keys = list(jax.random.split(key, 40))

    def nrm(i, shape, s):
        return jax.random.normal(keys[i], shape, jnp.float32) * s

    L = DEPTH
    return {
        "x": nrm(0, (BATCH, SEQ, D_MODEL), 1.0),
        "c": nrm(1, (BATCH, D_MODEL), 1.0),
        "ctx": nrm(2, (BATCH, CTX_LEN, D_MODEL), 1.0),
        "c_ctx": nrm(3, (D_MODEL,), 1.0),
        "ada_w": nrm(4, (L, D_MODEL, N_MOD * D_MODEL), 0.5 * D_MODEL ** -0.5),
        "ada_b": nrm(5, (L, N_MOD * D_MODEL), 0.01),
        "ffn1_w_gu": nrm(6, (L, D_MODEL, 2 * D_FF), D_MODEL ** -0.5),
        "ffn1_w_down": nrm(7, (L, D_FF, D_MODEL), BETA * D_FF ** -0.5),
        "ffn2_w_gu": nrm(8, (L, D_MODEL, 2 * D_FF), D_MODEL ** -0.5),
        "ffn2_w_down": nrm(9, (L, D_FF, D_MODEL), BETA * D_FF ** -0.5),
        "ln_g": 1.0 + nrm(10, (L, 3, D_MODEL), 0.01),
        "ln_b": nrm(11, (L, 3, D_MODEL), 0.01),
        "w_in": nrm(12, (L, D_MODEL, P_IN), D_MODEL ** -0.5),
        "w_out": nrm(13, (L, D_MIX, D_MODEL), BETA * D_MIX ** -0.5),
        "a_q_norm": 1.0 + nrm(14, (L, HEAD_DIM), 0.01),
        "a_k_norm": 1.0 + nrm(15, (L, HEAD_DIM), 0.01),
        "hy_conv_w": nrm(16, (L, 3, B_COLS), 3 ** -0.5),
        "hy_conv_b": nrm(17, (L, B_COLS), 0.01),
        "hy_f_w1": nrm(18, (L, HY_EMB, HY_ORDER), HY_EMB ** -0.5),
        "hy_f_b1": nrm(19, (L, HY_ORDER), 0.01),
        "hy_f_w2": nrm(20, (L, HY_ORDER, HY_ORDER), HY_ORDER ** -0.5),
        "hy_f_b2": nrm(21, (L, HY_ORDER), 0.01),
        "hy_f_w3": nrm(22, (L, HY_ORDER, HY_ORDER), HY_ORDER ** -0.5),
        "hy_f_b3": nrm(23, (L, HY_ORDER), 0.01),
        "hy_f_w4": nrm(24, (L, HY_ORDER, 2 * HY_WIDTH), HY_ORDER ** -0.5),
        "hy_f_freq": 1.0 + nrm(25, (L, HY_ORDER), 0.01),
        "hy_bias": nrm(26, (L, HY_WIDTH), 0.1),
        "mla_q_norm": 1.0 + nrm(27, (L, C_Q_LORA), 0.01),
        "mla_kv_norm": 1.0 + nrm(28, (L, C_KV_LORA), 0.01),
        "mla_w_uq": nrm(29, (L, C_Q_LORA, C_HEADS * (C_NOPE + C_ROPE)), C_Q_LORA ** -0.5),
        "mla_w_ukv": nrm(30, (L, C_KV_LORA, C_HEADS * (C_NOPE + C_V)), C_KV_LORA ** -0.5),
    }


def reference(x, c, ctx, c_ctx, ada_w, ada_b, ffn1_w_gu, ffn1_w_down, ffn2_w_gu, ffn2_w_down,
              ln_g, ln_b, w_in, w_out, a_q_norm, a_k_norm, hy_conv_w, hy_conv_b,
              hy_f_w1, hy_f_b1, hy_f_w2, hy_f_b2, hy_f_w3, hy_f_b3, hy_f_w4, hy_f_freq, hy_bias,
              mla_q_norm, mla_kv_norm, mla_w_uq, mla_w_ukv):
    b, n_lat, _ = x.shape
    rows = n_lat // GRID_W
    rope_a = axial_rope(rows, HEAD_DIM)
    rope_c = axial_rope(rows, C_ROPE)
    for l in range(DEPTH):
        need_ctx = l < DEPTH - 1
        m_lat = (jax.nn.silu(c) @ ada_w[l] + ada_b[l]).reshape(b, N_MOD, 1, D_MODEL)
        m_ctx = (jax.nn.silu(c_ctx) @ ada_w[l] + ada_b[l]).reshape(1, N_MOD, 1, D_MODEL)
        x = ffn_sublayer(x, m_lat[:, 0:3], ffn1_w_gu[l], ffn1_w_down[l], ln_g[l, 0], ln_b[l, 0])
        ctx = ffn_sublayer(ctx, m_ctx[:, 0:3], ffn1_w_gu[l], ffn1_w_down[l], ln_g[l, 0], ln_b[l, 0])
        hy = (hy_conv_w[l], hy_conv_b[l], hy_f_w1[l], hy_f_b1[l], hy_f_w2[l], hy_f_b2[l],
              hy_f_w3[l], hy_f_b3[l], hy_f_w4[l], hy_f_freq[l], hy_bias[l])
        mla = (mla_q_norm[l], mla_kv_norm[l], mla_w_uq[l], mla_w_ukv[l])
        mix_lat, mix_ctx = token_mix(modulate(x, m_lat[:, 3], m_lat[:, 4]),
                                     modulate(ctx, m_ctx[:, 3], m_ctx[:, 4]),
                                     w_in[l], w_out[l], a_q_norm[l], a_k_norm[l], hy, mla,
                                     rope_a, rope_c, need_ctx)
        x = layer_norm(ALPHA * x + m_lat[:, 5] * mix_lat, ln_g[l, 1], ln_b[l, 1])
        x = ffn_sublayer(x, m_lat[:, 6:9], ffn2_w_gu[l], ffn2_w_down[l], ln_g[l, 2], ln_b[l, 2])
        if need_ctx:
            ctx = layer_norm(ALPHA * ctx + m_ctx[:, 5] * mix_ctx, ln_g[l, 1], ln_b[l, 1])
            ctx = ffn_sublayer(ctx, m_ctx[:, 6:9], ffn2_w_gu[l], ffn2_w_down[l], ln_g[l, 2], ln_b[l, 2])
    return x
```

```python
import functools
import math

import numpy as np
import jax
import jax.numpy as jnp
from jax import lax
from jax.experimental import pallas as pl
from jax.experimental.pallas import tpu as pltpu

F32 = jnp.float32
BF16 = jnp.bfloat16

D_MODEL = 1024
GRID_W = 64
N_MOD = 9
D_FF = 2816
DEPTH = 2
ALPHA = (2 * DEPTH) ** 0.25
EPS = 1e-6
A_HEADS = 6
A_KV_HEADS = 2
HEAD_DIM = 64
ROPE_THETA = 10000.0
HY_WIDTH = 256
HY_EMB = 33
HY_BANDS = (HY_EMB - 1) // 2
HY_ORDER = 64
HY_TARGET = 1e-2
HY_FAST = 0.3
HY_SLOW = 1.5
C_HEADS = 6
C_Q_LORA = 256
C_KV_LORA = 128
C_NOPE = 64
C_ROPE = 32
C_V = 64
A_WIDTH = A_HEADS * HEAD_DIM
C_WIDTH = C_HEADS * C_V
A_COLS = (A_HEADS + 2 * A_KV_HEADS) * HEAD_DIM
B_COLS = 3 * HY_WIDTH
C_COLS = C_Q_LORA + C_KV_LORA + C_ROPE

LANES = 128
VMEM_LIMIT = 56 * 1024 * 1024

P_COLS = A_COLS + B_COLS + C_Q_LORA + C_KV_LORA + LANES
OFF_K = A_WIDTH
OFF_V = A_WIDTH + A_KV_HEADS * HEAD_DIM
OFF_B = A_COLS
OFF_CQ = A_COLS + B_COLS
OFF_CKV = OFF_CQ + C_Q_LORA
OFF_KR = OFF_CKV + C_KV_LORA
A_HEAD_ORDER = (0, 3, 1, 4, 2, 5)
CQ_PAD = C_HEADS * LANES


def _cparams(sem):
    return pltpu.CompilerParams(dimension_semantics=sem, vmem_limit_bytes=VMEM_LIMIT)


def _layer_norm(x):
    mu = jnp.mean(x, -1, keepdims=True)
    xc = x - mu
    var = jnp.mean(xc * xc, -1, keepdims=True)
    return xc * lax.rsqrt(var + EPS)


def _silu(x):
    return x * jax.nn.sigmoid(x)


def _dot(a, b):
    return jnp.dot(a, b, preferred_element_type=F32)


def _dot_nt(a, b):
    return lax.dot_general(a, b, (((1,), (1,)), ((), ())), preferred_element_type=F32)


def _dot_hp(a, b):
    return jnp.dot(a, b, preferred_element_type=F32, precision=lax.Precision.HIGHEST)


def _ada_kernel(c_ref, w_ref, b_ref, o_ref):
    a = _silu(c_ref[...]).astype(BF16)
    o_ref[0] = _dot(a, w_ref[0].astype(BF16)) + b_ref[0]


def _ada_mod(c_all, ada_w, ada_b):
    depth, d, n = ada_w.shape
    rows = c_all.shape[0]
    tn = n // 8
    return pl.pallas_call(
        _ada_kernel,
        grid=(depth, n // tn),
        in_specs=[
            pl.BlockSpec((rows, d), lambda l, j: (0, 0)),
            pl.BlockSpec((1, d, tn), lambda l, j: (l, 0, j)),
            pl.BlockSpec((1, 1, tn), lambda l, j: (l, 0, j)),
        ],
        out_specs=pl.BlockSpec((1, rows, tn), lambda l, j: (l, 0, j)),
        out_shape=jax.ShapeDtypeStruct((depth, rows, n), F32),
        compiler_params=_cparams(("arbitrary", "arbitrary")),
        name="ada_mod",
    )(c_all, ada_w, ada_b.reshape(depth, 1, n))


def _ffn_kernel(x_ref, mod_ref, wg_ref, wu_ref, wd_ref, g_ref, b_ref, o_ref, *, ck):
    x = x_ref[...]
    mod = mod_ref[0, 0]
    h = (_layer_norm(x) * (1.0 + mod[1:2]) + mod[0:1]).astype(BF16)
    acc = None
    for k in range(wg_ref.shape[1] // ck):
        g = _dot(h, wg_ref[:, k * ck:(k + 1) * ck])
        u = _dot(h, wu_ref[:, k * ck:(k + 1) * ck])
        a = (_silu(g) * u).astype(BF16)
        part = _dot(a, wd_ref[k * ck:(k + 1) * ck, :])
        acc = part if acc is None else acc + part
    y = ALPHA * x + (0.5 * mod[2:3]) * acc
    o_ref[...] = _layer_norm(y) * g_ref[...] + b_ref[...]


def _ffn_sublayer(x2, mods4, mod_group, batch_of_tile, wg, wu, wd, g, b, tm):
    rows, d = x2.shape
    f = wg.shape[1]
    const = lambda i: (0, 0)
    return pl.pallas_call(
        functools.partial(_ffn_kernel, ck=f // 2),
        grid=(rows // tm,),
        in_specs=[
            pl.BlockSpec((tm, d), lambda i: (i, 0)),
            pl.BlockSpec((1, 1, 3, d), lambda i: (batch_of_tile(i), mod_group, 0, 0)),
            pl.BlockSpec((d, f), const, pipeline_mode=pl.Buffered(1)),
            pl.BlockSpec((d, f), const, pipeline_mode=pl.Buffered(1)),
            pl.BlockSpec((f, d), const, pipeline_mode=pl.Buffered(1)),
            pl.BlockSpec((1, d), const),
            pl.BlockSpec((1, d), const),
        ],
        out_specs=pl.BlockSpec((tm, d), lambda i: (i, 0)),
        out_shape=jax.ShapeDtypeStruct((rows, d), F32),
        compiler_params=_cparams(("arbitrary",)),
        name="ffn_sublayer",
    )(x2, mods4, wg, wu, wd, g.reshape(1, d), b.reshape(1, d))


def _rope(x, cos, sin_signed, half):
    lane = lax.broadcasted_iota(jnp.int32, x.shape, 1)
    first = (lane % (2 * half)) < half
    partner = jnp.where(first, pltpu.roll(x, LANES - half, 1), pltpu.roll(x, half, 1))
    return x * cos + partner * sin_signed


def _half_tile_mean_sq(x, e_ref):
    sq = x * x
    hi = sq.astype(BF16)
    lo = (sq - hi.astype(F32)).astype(BF16)
    e = e_ref[...]
    return (_dot(hi, e) + _dot(lo, e)) * (1.0 / HEAD_DIM)


def _mix_in_kernel(x_ref, mod_ref, w_ref, e_ref, gq_ref, gk_ref, cq_g_ref, ckv_g_ref, wuq_ref, wuk_ref, wuv_ref,
                   *rest, use_rope):
    if use_rope:
        cos_a_ref, sin_a_ref, cos_c_ref, sin_c_ref = rest[:4]
        rest = rest[4:]
    qa_ref, ka_ref, va_ref, pb_ref, qc_ref, kc_ref, vc_ref = rest

    mod = mod_ref[0, 0]
    h = (_layer_norm(x_ref[...]) * (1.0 + mod[1:2]) + mod[0:1]).astype(BF16)
    p = _dot(h, w_ref[...])

    def rope_a(t):
        return _rope(t, cos_a_ref[...], sin_a_ref[...], HEAD_DIM // 2) if use_rope else t

    def rope_c(t):
        return _rope(t, cos_c_ref[...], sin_c_ref[...], C_ROPE // 2) if use_rope else t

    for c in range(A_WIDTH // LANES):
        t = p[:, c * LANES:(c + 1) * LANES]
        t = t * lax.rsqrt(_half_tile_mean_sq(t, e_ref) + EPS) * gq_ref[...]
        qa_ref[:, c * LANES:(c + 1) * LANES] = (rope_a(t) * HEAD_DIM ** -0.5).astype(BF16)
    t = p[:, OFF_K:OFF_K + LANES]
    t = t * lax.rsqrt(_half_tile_mean_sq(t, e_ref) + EPS) * gk_ref[...]
    ka_ref[...] = rope_a(t).astype(BF16)
    v = p[:, OFF_V:OFF_V + LANES]
    lane = lax.broadcasted_iota(jnp.int32, v.shape, 1)
    low = lane < HEAD_DIM
    va_ref[:, 0:LANES] = jnp.where(low, v, 1.0).astype(BF16)
    va_ref[:, LANES:2 * LANES] = jnp.where(low, 1.0, v).astype(BF16)

    pb_ref[...] = p[:, OFF_B:OFF_B + B_COLS]

    c_q = p[:, OFF_CQ:OFF_CQ + C_Q_LORA]
    c_q = c_q * lax.rsqrt(jnp.mean(c_q * c_q, -1, keepdims=True) + EPS) * cq_g_ref[...]
    qq = _dot(c_q.astype(BF16), wuq_ref[...])
    c_kv = p[:, OFF_CKV:OFF_CKV + C_KV_LORA]
    c_kv = (c_kv * lax.rsqrt(jnp.mean(c_kv * c_kv, -1, keepdims=True) + EPS) * ckv_g_ref[...]).astype(BF16)
    kn = _dot(c_kv, wuk_ref[...])
    vv = _dot(c_kv, wuv_ref[...])
    k_r = rope_c(p[:, OFF_KR:OFF_KR + LANES])
    scale_c = (C_NOPE + C_ROPE) ** -0.5
    for hd in range(C_HEADS):
        sl = slice(hd * LANES, (hd + 1) * LANES)
        qc_ref[:, sl] = (rope_c(qq[:, sl]) * scale_c).astype(BF16)
        kc_ref[:, sl] = (kn[:, sl] + k_r).astype(BF16)
        ones_low = (hd % 2) == 1
        vc_ref[:, sl] = jnp.where(low != ones_low, vv[:, sl], 1.0).astype(BF16)


def _mix_in(x2, mods4, batch_of_tile, wts, rope_tabs, tiles_per_seq, tm):
    rows, d = x2.shape
    const = lambda i: (0, 0)
    use_rope = rope_tabs is not None
    in_specs = [
        pl.BlockSpec((tm, d), lambda i: (i, 0)),
        pl.BlockSpec((1, 1, 3, d), lambda i: (batch_of_tile(i), 1, 0, 0)),
        pl.BlockSpec((d, P_COLS), const, pipeline_mode=pl.Buffered(1)),
        pl.BlockSpec((LANES, LANES), const),
        pl.BlockSpec((1, LANES), const),
        pl.BlockSpec((1, LANES), const),
        pl.BlockSpec((1, C_Q_LORA), const),
        pl.BlockSpec((1, C_KV_LORA), const),
        pl.BlockSpec((C_Q_LORA, CQ_PAD), const),
        pl.BlockSpec((C_KV_LORA, CQ_PAD), const),
        pl.BlockSpec((C_KV_LORA, CQ_PAD), const),
    ]
    args = [x2, mods4, wts["w_in"], wts["e"], wts["gq"], wts["gk"], wts["cq_g"], wts["ckv_g"],
            wts["w_uq"], wts["w_uk"], wts["w_uv"]]
    if use_rope:
        in_specs += [pl.BlockSpec((tm, LANES), lambda i: (i % tiles_per_seq, 0))] * 4
        args += list(rope_tabs)
    row_spec = lambda w: pl.BlockSpec((tm, w), lambda i: (i, 0))
    widths = (A_WIDTH, LANES, 2 * LANES, B_COLS, CQ_PAD, CQ_PAD, CQ_PAD)
    dtypes = (BF16, BF16, BF16, F32, BF16, BF16, BF16)
    return pl.pallas_call(
        functools.partial(_mix_in_kernel, use_rope=use_rope),
        grid=(rows // tm,),
        in_specs=in_specs,
        out_specs=[row_spec(w) for w in widths],
        out_shape=[jax.ShapeDtypeStruct((rows, w), dt) for w, dt in zip(widths, dtypes)],
        compiler_params=_cparams(("arbitrary",)),
        name="mix_in_rope" if use_rope else "mix_in",
    )(*args)


def _flash(q, sources, tk):
    m = None
    acc = None

    def step(k, v, m, acc):
        s = _dot_nt(q, k)
        m_tile = jnp.max(s, axis=-1, keepdims=True)
        if m is None:
            p = jnp.exp(s - m_tile).astype(BF16)
            return m_tile, _dot(p, v)
        m_new = jnp.maximum(m, m_tile)
        p = jnp.exp(s - m_new).astype(BF16)
        return m_new, jnp.exp(m - m_new) * acc + _dot(p, v)

    for k_get, v_get, n_rows in sources:
        chunk = min(tk, n_rows)
        n_chunks = n_rows // chunk
        start = 0
        if m is None:
            m, acc = step(k_get(0, chunk), v_get(0, chunk), None, None)
            start = 1
        if n_chunks > start:
            def body(i, carry, k_get=k_get, v_get=v_get, chunk=chunk):
                off = pl.multiple_of(i * chunk, chunk)
                return step(k_get(off, chunk), v_get(off, chunk), *carry)
            m, acc = lax.fori_loop(start, n_chunks, body, (m, acc))
    return acc


def _merge_halves(acc_low, acc_high):
    lane = lax.broadcasted_iota(jnp.int32, acc_low.shape, 1)
    low = lane < HEAD_DIM
    num = jnp.where(low, acc_low, acc_high)
    den = pltpu.roll(jnp.where(low, acc_high, acc_low), HEAD_DIM, 1)
    return num / den


def _gqa_kernel(q_ref, *refs, n_src, tk):
    kv_refs, o_ref = refs[:2 * n_src], refs[2 * n_src]
    tq = q_ref.shape[1]
    n_tiles = A_WIDTH // LANES
    lane = lax.broadcasted_iota(jnp.int32, (tq, LANES), 1)
    low = lane < HEAD_DIM
    accs = []
    for g in range(A_KV_HEADS):
        keep = low if g == 0 else jnp.logical_not(low)
        zero = jnp.zeros((tq, LANES), BF16)
        q = jnp.concatenate(
            [jnp.where(keep, q_ref[0, :, c * LANES:(c + 1) * LANES], zero) for c in range(n_tiles)], axis=0)
        sources = []
        for s in range(n_src):
            k_ref, v_ref = kv_refs[2 * s], kv_refs[2 * s + 1]
            sources.append((
                lambda off, size, k_ref=k_ref: k_ref[0, pl.ds(off, size), :],
                lambda off, size, v_ref=v_ref, g=g: v_ref[0, pl.ds(off, size), g * LANES:(g + 1) * LANES],
                k_ref.shape[1]))
        accs.append(_flash(q, sources, tk))
    for c in range(n_tiles):
        rows = slice(c * tq, (c + 1) * tq)
        o_ref[0, :, c * LANES:(c + 1) * LANES] = _merge_halves(accs[0][rows], accs[1][rows]).astype(BF16)


def _gqa_attention(q, kv_list, tq, tk):
    b, lq, _ = q.shape
    in_specs = [pl.BlockSpec((1, tq, A_WIDTH), lambda i, j: (i, j, 0))]
    args = [q]
    for k, v in kv_list:
        in_specs.append(pl.BlockSpec((1, k.shape[1], LANES), lambda i, j: (i, 0, 0)))
        in_specs.append(pl.BlockSpec((1, v.shape[1], 2 * LANES), lambda i, j: (i, 0, 0)))
        args += [k, v]
    return pl.pallas_call(
        functools.partial(_gqa_kernel, n_src=len(kv_list), tk=tk),
        grid=(b, lq // tq),
        in_specs=in_specs,
        out_specs=pl.BlockSpec((1, tq, A_WIDTH), lambda i, j: (i, j, 0)),
        out_shape=jax.ShapeDtypeStruct((b, lq, A_WIDTH), BF16),
        compiler_params=_cparams(("arbitrary", "arbitrary")),
        name="gqa_attention",
    )(*args)


def _mla_kernel(q_ref, *refs, n_src, tk):
    kv_refs, o_ref = refs[:2 * n_src], refs[2 * n_src]
    accs = []
    for e in range(2):
        sl = slice(e * LANES, (e + 1) * LANES)
        sources = []
        for s in range(n_src):
            k_ref, v_ref = kv_refs[2 * s], kv_refs[2 * s + 1]
            sources.append((
                lambda off, size, k_ref=k_ref, sl=sl: k_ref[0, pl.ds(off, size), sl],
                lambda off, size, v_ref=v_ref, sl=sl: v_ref[0, pl.ds(off, size), sl],
                k_ref.shape[1]))
        accs.append(_flash(q_ref[0, :, sl], sources, tk))
    o_ref[0] = _merge_halves(accs[0], accs[1]).astype(BF16)


def _mla_attention(q, kv_list, tq, tk):
    b, lq, _ = q.shape
    in_specs = [pl.BlockSpec((1, tq, 2 * LANES), lambda i, hp, j: (i, j, hp))]
    args = [q]
    for k, v in kv_list:
        in_specs.append(pl.BlockSpec((1, k.shape[1], 2 * LANES), lambda i, hp, j: (i, 0, hp)))
        in_specs.append(pl.BlockSpec((1, v.shape[1], 2 * LANES), lambda i, hp, j: (i, 0, hp)))
        args += [k, v]
    return pl.pallas_call(
        functools.partial(_mla_kernel, n_src=len(kv_list), tk=tk),
        grid=(b, C_HEADS // 2, lq // tq),
        in_specs=in_specs,
        out_specs=pl.BlockSpec((1, tq, LANES), lambda i, hp, j: (i, j, hp)),
        out_shape=jax.ShapeDtypeStruct((b, lq, C_WIDTH), BF16),
        compiler_params=_cparams(("arbitrary", "arbitrary", "arbitrary")),
        name="mla_attention",
    )(*args)


def _fft_dims(seq):
    n = 2 * seq
    n2 = min(LANES, n // 16)
    return n // n2, n2


def _stack_real(m):
    return np.block([[m.real, -m.imag], [m.imag, m.real]]).astype(np.float32)


@functools.lru_cache(maxsize=None)
def _fft_constants(seq):
    n1, n2 = _fft_dims(seq)
    n = n1 * n2
    k1 = np.arange(n1)[:, None]
    f1 = np.exp(-2j * np.pi * k1 * np.arange(n1 // 2)[None, :] / n1)
    f2 = np.exp(-2j * np.pi * np.arange(n2)[:, None] * np.arange(n2)[None, :] / n2)
    tw = np.exp(-2j * np.pi * k1 * np.arange(n2)[None, :] / n)
    g1 = np.conj(f1).T / n
    return dict(
        m1=_stack_real(f1), m2=_stack_real(f2), m2_inv=_stack_real(np.conj(f2)), m1_inv=_stack_real(g1),
        tw_r=tw.real.astype(np.float32)[:, :, None], tw_i=tw.imag.astype(np.float32)[:, :, None])


def _hy_prep_kernel(v_ref, x1_ref, x0_ref, wv_ref, wx1_ref, wx0_ref, bv_ref, bx1_ref, bx0_ref, u_ref, x0c_ref):
    seq = v_ref.shape[1]
    row = lax.broadcasted_iota(jnp.int32, (seq, LANES), 0)
    first, last = row == 0, row == seq - 1

    def conv(p_ref, w_ref, b_ref):
        t = p_ref[0]
        prev = jnp.where(first, 0.0, pltpu.roll(t, 1, 0))
        nxt = jnp.where(last, 0.0, pltpu.roll(t, seq - 1, 0))
        return prev * w_ref[0:1] + t * w_ref[1:2] + nxt * w_ref[2:3] + b_ref[...]

    u_ref[0] = conv(v_ref, wv_ref, bv_ref) * conv(x1_ref, wx1_ref, bx1_ref)
    x0c_ref[0] = conv(x0_ref, wx0_ref, bx0_ref)


def _hy_prep(pb, conv_w, conv_b):
    b, seq, _ = pb.shape
    n_c = HY_WIDTH // LANES
    p_spec = lambda part: pl.BlockSpec((1, seq, LANES), lambda i, c: (i, 0, part * n_c + c))
    w_spec = lambda part: pl.BlockSpec((3, LANES), lambda i, c: (0, part * n_c + c))
    b_spec = lambda part: pl.BlockSpec((1, LANES), lambda i, c: (0, part * n_c + c))
    out_spec = pl.BlockSpec((1, seq, LANES), lambda i, c: (i, 0, c))
    cb = conv_b.reshape(1, B_COLS)
    return pl.pallas_call(
        _hy_prep_kernel,
        grid=(b, n_c),
        in_specs=[p_spec(0), p_spec(1), p_spec(2), w_spec(0), w_spec(1), w_spec(2), b_spec(0), b_spec(1), b_spec(2)],
        out_specs=[out_spec, out_spec],
        out_shape=[jax.ShapeDtypeStruct((b, seq, HY_WIDTH), F32)] * 2,
        compiler_params=_cparams(("arbitrary", "arbitrary")),
        name="hyena_prep",
    )(pb, pb, pb, conv_w, conv_w, conv_w, cb, cb, cb)


def _hy_stage1_kernel(u_ref, m1_ref, a_ref):
    _, h, lc = u_ref.shape
    z = u_ref[...].reshape(2 * h, lc).astype(BF16)
    a_ref[0] = _dot(m1_ref[...].astype(BF16), z).reshape(2, 2 * h, lc)


def _hy_stage1(u, seq):
    b, _, ch = u.shape
    n1, n2 = _fft_dims(seq)
    cols = n2 * ch
    lc = min(cols, 4096)
    m1 = jnp.asarray(_fft_constants(seq)["m1"])
    return pl.pallas_call(
        _hy_stage1_kernel,
        grid=(b // 2, cols // lc),
        in_specs=[pl.BlockSpec((2, n1 // 2, lc), lambda j, c: (j, 0, c)),
                  pl.BlockSpec((2 * n1, n1), lambda j, c: (0, 0))],
        out_specs=pl.BlockSpec((1, 2, n1, lc), lambda j, c: (j, 0, 0, c)),
        out_shape=jax.ShapeDtypeStruct((b // 2, 2, n1, cols), F32),
        compiler_params=_cparams(("arbitrary", "arbitrary")),
        name="hyena_stage1",
    )(u.reshape(b, n1 // 2, cols), m1)


def _twiddle(ar, ai, tr, ti):
    return ar * tr - ai * ti, ar * ti + ai * tr


def _hy_filter_spec_kernel(a_ref, twr_ref, twi_ref, m2_ref, inv_ref, h_ref):
    n2 = a_ref.shape[3]
    m2 = m2_ref[...].astype(BF16)
    tr, ti = twr_ref[0], twi_ref[0]

    def fwd(pair):
        br, bi = _twiddle(a_ref[pair, 0, 0], a_ref[pair, 1, 0], tr, ti)
        return _dot(m2, jnp.concatenate([br, bi], axis=0).astype(BF16))

    zf, zb = fwd(0), fwd(1)
    h_ref[0, 0:n2] = (zf[0:n2] + zb[0:n2]) * inv_ref[...]
    h_ref[0, n2:2 * n2] = (zf[n2:2 * n2] - zb[n2:2 * n2]) * inv_ref[...]


def _hy_filter_spec(a, inv_norm, seq):
    n1, n2 = _fft_dims(seq)
    cst = _fft_constants(seq)
    a5 = a.reshape(2, 2, n1, n2, HY_WIDTH)
    return pl.pallas_call(
        _hy_filter_spec_kernel,
        grid=(n1,),
        in_specs=[pl.BlockSpec((2, 2, 1, n2, HY_WIDTH), lambda k: (0, 0, k, 0, 0)),
                  pl.BlockSpec((1, n2, 1), lambda k: (k, 0, 0)),
                  pl.BlockSpec((1, n2, 1), lambda k: (k, 0, 0)),
                  pl.BlockSpec((2 * n2, 2 * n2), lambda k: (0, 0)),
                  pl.BlockSpec((1, HY_WIDTH), lambda k: (0, 0))],
        out_specs=pl.BlockSpec((1, 2 * n2, HY_WIDTH), lambda k: (k, 0, 0)),
        out_shape=jax.ShapeDtypeStruct((n1, 2 * n2, HY_WIDTH), F32),
        compiler_params=_cparams(("arbitrary",)),
        name="hyena_filter_spectrum",
    )(a5, jnp.asarray(cst["tw_r"]), jnp.asarray(cst["tw_i"]), jnp.asarray(cst["m2"]), inv_norm)


def _hy_mid_kernel(a_ref, twr_ref, twi_ref, m2_ref, m2i_ref, h_ref, q_ref):
    n2 = a_ref.shape[3]
    tr, ti = twr_ref[0], twi_ref[0]
    br, bi = _twiddle(a_ref[0, 0, 0], a_ref[0, 1, 0], tr, ti)
    z = _dot(m2_ref[...].astype(BF16), jnp.concatenate([br, bi], axis=0).astype(BF16))
    zr, zi = z[0:n2], z[n2:2 * n2]
    hr, hi = h_ref[0, 0:n2], h_ref[0, n2:2 * n2]
    yr, yi = zr * hr - zi * hi, zr * hi + zi * hr
    p = _dot(m2i_ref[...].astype(BF16), jnp.concatenate([yr, yi], axis=0).astype(BF16))
    qr, qi = _twiddle(p[0:n2], p[n2:2 * n2], tr, -ti)
    q_ref[0, 0, 0] = qr
    q_ref[0, 1, 0] = qi


def _hy_mid(a, h_spec, seq):
    pairs = a.shape[0]
    n1, n2 = _fft_dims(seq)
    cst = _fft_constants(seq)
    a5 = a.reshape(pairs, 2, n1, n2, HY_WIDTH)
    blk = pl.BlockSpec((1, 2, 1, n2, HY_WIDTH), lambda k, j: (j, 0, k, 0, 0))
    q = pl.pallas_call(
        _hy_mid_kernel,
        grid=(n1, pairs),
        in_specs=[blk,
                  pl.BlockSpec((1, n2, 1), lambda k, j: (k, 0, 0)),
                  pl.BlockSpec((1, n2, 1), lambda k, j: (k, 0, 0)),
                  pl.BlockSpec((2 * n2, 2 * n2), lambda k, j: (0, 0)),
                  pl.BlockSpec((2 * n2, 2 * n2), lambda k, j: (0, 0)),
                  pl.BlockSpec((1, 2 * n2, HY_WIDTH), lambda k, j: (k, 0, 0))],
        out_specs=blk,
        out_shape=jax.ShapeDtypeStruct((pairs, 2, n1, n2, HY_WIDTH), F32),
        compiler_params=_cparams(("arbitrary", "arbitrary")),
        name="hyena_mid",
    )(a5, jnp.asarray(cst["tw_r"]), jnp.asarray(cst["tw_i"]), jnp.asarray(cst["m2"]), jnp.asarray(cst["m2_inv"]),
      h_spec)
    return q.reshape(pairs, 2, n1, n2 * HY_WIDTH)


def _hy_out_kernel(q_ref, m1i_ref, u_ref, x0_ref, d_ref, o_ref):
    _, _, n1, lc = q_ref.shape
    q = q_ref[0].reshape(2 * n1, lc).astype(BF16)
    y = _dot(m1i_ref[...].astype(BF16), q).reshape(2, n1 // 2, lc)
    o_ref[...] = ((y + u_ref[...] * d_ref[...]) * x0_ref[...]).astype(o_ref.dtype)


def _hy_out(q, u, x0c, d_skip, seq):
    b, _, ch = u.shape
    n1, n2 = _fft_dims(seq)
    cols = n2 * ch
    lc = min(cols, 4096)
    m1i = jnp.asarray(_fft_constants(seq)["m1_inv"])
    blk = pl.BlockSpec((2, n1 // 2, lc), lambda j, c: (j, 0, c))
    out = pl.pallas_call(
        _hy_out_kernel,
        grid=(b // 2, cols // lc),
        in_specs=[pl.BlockSpec((1, 2, n1, lc), lambda j, c: (j, 0, 0, c)),
                  pl.BlockSpec((n1, 2 * n1), lambda j, c: (0, 0)),
                  blk, blk,
                  pl.BlockSpec((1, 1, lc), lambda j, c: (0, 0, c))],
        out_specs=blk,
        out_shape=jax.ShapeDtypeStruct((b, n1 // 2, cols), BF16),
        compiler_params=_cparams(("arbitrary", "arbitrary")),
        name="hyena_out",
    )(q, m1i, u.reshape(b, n1 // 2, cols), x0c.reshape(b, n1 // 2, cols),
      jnp.tile(d_skip, n2).reshape(1, 1, cols))
    return out.reshape(b, seq, ch)


def _hy_filter_kernel(z_ref, t_ref, w1_ref, b1_ref, w2_ref, b2_ref, w3_ref, b3_ref, w4_ref, fr_ref, dl_ref,
                      taps_ref, inv_ref):
    fr = fr_ref[...]
    hdn = jnp.sin(fr * (_dot_hp(z_ref[...], w1_ref[...]) + b1_ref[...]))
    hdn = jnp.sin(fr * (_dot_hp(hdn, w2_ref[...]) + b2_ref[...]))
    hdn = jnp.sin(fr * (_dot_hp(hdn, w3_ref[...]) + b3_ref[...]))
    h = _dot_hp(hdn, w4_ref[...]) * jnp.exp(-t_ref[...] * dl_ref[...])
    h_fwd, h_bwd = h[:, :HY_WIDTH], h[:, HY_WIDTH:]
    row = lax.broadcasted_iota(jnp.int32, h_bwd.shape, 0)
    h_bwd = jnp.where(row == 0, 0.0, h_bwd)
    norm = jnp.sum(jnp.abs(h_fwd), 0, keepdims=True) + jnp.sum(jnp.abs(h_bwd), 0, keepdims=True)
    inv_ref[...] = 1.0 / norm
    zeros = jnp.zeros_like(h_fwd)
    taps_ref[0] = h_fwd
    taps_ref[1] = zeros
    taps_ref[2] = h_bwd
    taps_ref[3] = zeros


def _hy_filter(seq, w1, b1, w2, b2, w3, b3, w4, freq):
    t = jnp.linspace(0.0, 1.0, seq, dtype=F32)[:, None]
    w = 2.0 * math.pi * jnp.arange(seq, dtype=F32)[:, None] / seq
    f = jnp.linspace(1e-4, HY_BANDS - 1, HY_BANDS, dtype=F32)[None, :]
    z = jnp.concatenate([t, jnp.cos(f * w), -jnp.sin(f * w)], -1)
    z = jnp.pad(z, ((0, 0), (0, HY_ORDER - HY_EMB)))
    w1p = jnp.pad(w1, ((0, HY_ORDER - HY_EMB), (0, 0)))
    deltas = jnp.abs(jnp.linspace(math.log(HY_TARGET) / HY_SLOW, math.log(HY_TARGET) / HY_FAST, HY_WIDTH, dtype=F32))
    row = lambda v: v.reshape(1, -1)
    return pl.pallas_call(
        _hy_filter_kernel,
        out_shape=[jax.ShapeDtypeStruct((4, seq, HY_WIDTH), F32), jax.ShapeDtypeStruct((1, HY_WIDTH), F32)],
        compiler_params=pltpu.CompilerParams(vmem_limit_bytes=VMEM_LIMIT),
        name="hyena_filter",
    )(z, t, w1p, row(b1), w2, row(b2), w3, row(b3), w4, row(freq), row(jnp.tile(deltas, 2)))


def _hyena(pb, hy, seq):
    conv_w, conv_b, w1, b1, w2, b2, w3, b3, w4, freq, d_skip = hy
    taps, inv_norm = _hy_filter(seq, w1, b1, w2, b2, w3, b3, w4, freq)
    h_spec = _hy_filter_spec(_hy_stage1(taps, seq), inv_norm, seq)
    u, x0c = _hy_prep(pb, conv_w, conv_b)
    q = _hy_mid(_hy_stage1(u, seq), h_spec, seq)
    return _hy_out(q, u, x0c, d_skip, seq)


def _mix_out_kernel(x_ref, mod_ref, oa_ref, ob_ref, oc_ref, wa_ref, wb_ref, wc_ref, g_ref, b_ref, o_ref):
    mix = _dot(oa_ref[...], wa_ref[...]) + _dot(ob_ref[...], wb_ref[...]) + _dot(oc_ref[...], wc_ref[...])
    y = ALPHA * x_ref[...] + mod_ref[0, 0][2:3] * mix
    o_ref[...] = _layer_norm(y) * g_ref[...] + b_ref[...]


def _mix_out(x2, mods4, batch_of_tile, oa, ob, oc, wa, wb, wc, g, b, tm):
    rows, d = x2.shape
    const = lambda i: (0, 0)
    row_spec = lambda w: pl.BlockSpec((tm, w), lambda i: (i, 0))
    return pl.pallas_call(
        _mix_out_kernel,
        grid=(rows // tm,),
        in_specs=[row_spec(d),
                  pl.BlockSpec((1, 1, 3, d), lambda i: (batch_of_tile(i), 1, 0, 0)),
                  row_spec(A_WIDTH), row_spec(HY_WIDTH), row_spec(C_WIDTH),
                  pl.BlockSpec((A_WIDTH, d), const), pl.BlockSpec((HY_WIDTH, d), const),
                  pl.BlockSpec((C_WIDTH, d), const),
                  pl.BlockSpec((1, d), const), pl.BlockSpec((1, d), const)],
        out_specs=row_spec(d),
        out_shape=jax.ShapeDtypeStruct((rows, d), F32),
        compiler_params=_cparams(("arbitrary",)),
        name="mix_out",
    )(x2, mods4, oa, ob, oc, wa, wb, wc, g.reshape(1, d), b.reshape(1, d))


def _take_cols(w, idx):
    idx = np.asarray(idx)
    return jnp.where(jnp.asarray(idx >= 0)[None, :], w[:, np.maximum(idx, 0)], 0.0)


def _mix_weights(w_in, w_out, a_qn, a_kn, q_g, kv_g, w_uq, w_ukv):
    col = np.arange
    q_cols = np.concatenate([col(h * HEAD_DIM, (h + 1) * HEAD_DIM) for h in A_HEAD_ORDER])
    pad = lambda n: -np.ones(n, np.int64)
    in_idx = np.concatenate([q_cols, col(A_WIDTH, OFF_KR), pad(HEAD_DIM), col(OFF_KR, OFF_KR + C_ROPE),
                             pad(LANES - HEAD_DIM - C_ROPE)])
    dq = C_NOPE + C_ROPE
    uq_idx = np.concatenate([np.concatenate([col(h * dq, (h + 1) * dq), pad(LANES - dq)]) for h in range(C_HEADS)])
    dkv = C_NOPE + C_V
    uk_idx = np.concatenate([np.concatenate([col(h * dkv, h * dkv + C_NOPE), pad(LANES - C_NOPE)])
                             for h in range(C_HEADS)])
    v_cols = lambda h: col(h * dkv + C_NOPE, (h + 1) * dkv)
    uv_idx = np.concatenate([np.concatenate([v_cols(h), pad(LANES - C_V)] if h % 2 == 0 else
                                            [pad(LANES - C_V), v_cols(h)]) for h in range(C_HEADS)])
    half = np.arange(LANES) // HEAD_DIM
    e = (half[:, None] == half[None, :]).astype(np.float32)
    tile2 = lambda g: jnp.tile(g, LANES // HEAD_DIM).reshape(1, LANES)
    return dict(
        w_in=_take_cols(w_in, in_idx).astype(BF16),
        e=jnp.asarray(e, BF16),
        gq=tile2(a_qn), gk=tile2(a_kn),
        cq_g=q_g.reshape(1, -1), ckv_g=kv_g.reshape(1, -1),
        w_uq=_take_cols(w_uq, uq_idx).astype(BF16),
        w_uk=_take_cols(w_ukv, uk_idx).astype(BF16),
        w_uv=_take_cols(w_ukv, uv_idx).astype(BF16),
        wa=w_out[q_cols].astype(BF16),
        wb=w_out[A_WIDTH:A_WIDTH + HY_WIDTH].astype(BF16),
        wc=w_out[A_WIDTH + HY_WIDTH:].astype(BF16),
    )


def _rope_tables(seq):
    rows = seq // GRID_W
    row = jnp.repeat(jnp.arange(rows, dtype=F32), GRID_W)
    colv = jnp.tile(jnp.arange(GRID_W, dtype=F32), rows)

    def cos_sin(rot_dim):
        n_freq = rot_dim // 4
        inv = ROPE_THETA ** (-jnp.arange(n_freq, dtype=F32) / n_freq)
        ang = jnp.concatenate([row[:, None] * inv, colv[:, None] * inv], -1)
        return jnp.cos(ang), jnp.sin(ang)

    cos, sin = cos_sin(HEAD_DIM)
    cos_a = jnp.tile(cos, (1, 2 * LANES // HEAD_DIM))
    sin_a = jnp.tile(jnp.concatenate([-sin, sin], -1), (1, LANES // HEAD_DIM))
    cos, sin = cos_sin(C_ROPE)
    ones = jnp.ones((seq, C_NOPE), F32)
    tail = LANES - C_NOPE - C_ROPE
    cos_c = jnp.concatenate([ones, cos, cos, jnp.ones((seq, tail), F32)], -1)
    sin_c = jnp.concatenate([0.0 * ones, -sin, sin, jnp.zeros((seq, tail), F32)], -1)
    return cos_a, sin_a, cos_c, sin_c


def kernel(x, c, ctx, c_ctx, ada_w, ada_b, ffn1_w_gu, ffn1_w_down, ffn2_w_gu, ffn2_w_down, ln_g, ln_b, w_in, w_out,
           a_q_norm, a_k_norm, hy_conv_w, hy_conv_b, hy_f_w1, hy_f_b1, hy_f_w2, hy_f_b2, hy_f_w3, hy_f_b3, hy_f_w4,
           hy_f_freq, hy_bias, mla_q_norm, mla_kv_norm, mla_w_uq, mla_w_ukv):
    b, seq, d = x.shape
    n_ctx = ctx.shape[1]
    depth = ada_w.shape[0]
    assert b % 2 == 0 and seq % GRID_W == 0

    tm = min(512, seq)
    tm_ctx = min(512, b * n_ctx)
    tiles_per_seq = seq // tm
    lat_batch = lambda i: i // tiles_per_seq
    ctx_batch = lambda i: b

    mod_rows = ((b + 1 + 7) // 8) * 8
    c_all = jnp.concatenate([c, c_ctx[None], jnp.zeros((mod_rows - b - 1, d), F32)], 0)
    mods = _ada_mod(c_all, ada_w, ada_b).reshape(depth, mod_rows, N_MOD // 3, 3, d)

    rope_tabs = _rope_tables(seq)
    x2 = x.reshape(b * seq, d)
    ctx2 = ctx.reshape(b * n_ctx, d)
    tq_a, tq_c, tk = min(256, seq), min(512, seq), min(512, seq)

    for l in range(depth):
        need_ctx = l < depth - 1
        m4 = mods[l]
        ffn1 = (ffn1_w_gu[l, :, :D_FF].astype(BF16), ffn1_w_gu[l, :, D_FF:].astype(BF16),
                ffn1_w_down[l].astype(BF16))
        ffn2 = (ffn2_w_gu[l, :, :D_FF].astype(BF16), ffn2_w_gu[l, :, D_FF:].astype(BF16),
                ffn2_w_down[l].astype(BF16))
        wts = _mix_weights(w_in[l], w_out[l], a_q_norm[l], a_k_norm[l], mla_q_norm[l], mla_kv_norm[l],
                           mla_w_uq[l], mla_w_ukv[l])
        hy = (hy_conv_w[l], hy_conv_b[l], hy_f_w1[l], hy_f_b1[l], hy_f_w2[l], hy_f_b2[l], hy_f_w3[l], hy_f_b3[l],
              hy_f_w4[l], hy_f_freq[l], hy_bias[l])

        x2 = _ffn_sublayer(x2, m4, 0, lat_batch, *ffn1, ln_g[l, 0], ln_b[l, 0], tm)
        ctx2 = _ffn_sublayer(ctx2, m4, 0, ctx_batch, *ffn1, ln_g[l, 0], ln_b[l, 0], tm_ctx)

        qa, ka, va, pb, qc, kc, vc = _mix_in(x2, m4, lat_batch, wts, rope_tabs, tiles_per_seq, tm)
        qa_c, ka_c, va_c, pb_c, qc_c, kc_c, vc_c = _mix_in(ctx2, m4, ctx_batch, wts, None, 1, tm_ctx)
        r3 = lambda a, n: a.reshape(b, n, a.shape[-1])
        ka_c, va_c, kc_c, vc_c = (r3(a, n_ctx) for a in (ka_c, va_c, kc_c, vc_c))

        oa = _gqa_attention(r3(qa, seq), [(ka_c, va_c), (r3(ka, seq), r3(va, seq))], tq_a, tk)
        oc = _mla_attention(r3(qc, seq), [(kc_c, vc_c), (r3(kc, seq), r3(vc, seq))], tq_c, tk)
        ob = _hyena(r3(pb, seq), hy, seq)
        flat = lambda a: a.reshape(-1, a.shape[-1])
        x_mix = _mix_out(x2, m4, lat_batch, flat(oa), flat(ob), flat(oc), wts["wa"], wts["wb"], wts["wc"],
                         ln_g[l, 1], ln_b[l, 1], tm)
        if need_ctx:
            oa_c = _gqa_attention(r3(qa_c, n_ctx), [(ka_c, va_c)], n_ctx, tk)
            oc_c = _mla_attention(r3(qc_c, n_ctx), [(kc_c, vc_c)], n_ctx, tk)
            ob_c = _hyena(r3(pb_c, n_ctx), hy, n_ctx)
            ctx2 = _mix_out(ctx2, m4, ctx_batch, flat(oa_c), flat(ob_c), flat(oc_c), wts["wa"], wts["wb"],
                            wts["wc"], ln_g[l, 1], ln_b[l, 1], tm_ctx)
            ctx2 = _ffn_sublayer(ctx2, m4, 2, ctx_batch, *ffn2, ln_g[l, 2], ln_b[l, 2], tm_ctx)
        x2 = _ffn_sublayer(x_mix, m4, 2, lat_batch, *ffn2, ln_g[l, 2], ln_b[l, 2], tm)
    return x2.reshape(b, seq, d)
```

```python
import functools
import math

import numpy as np
import jax
import jax.numpy as jnp
from jax import lax
from jax.experimental import pallas as pl
from jax.experimental.pallas import tpu as pltpu

F32 = jnp.float32
BF16 = jnp.bfloat16

D_MODEL = 1024
GRID_W = 64
N_MOD = 9
D_FF = 2816
DEPTH = 2
ALPHA = (2 * DEPTH) ** 0.25
EPS = 1e-6
A_HEADS = 6
A_KV_HEADS = 2
HEAD_DIM = 64
ROPE_THETA = 10000.0
HY_WIDTH = 256
HY_EMB = 33
HY_BANDS = (HY_EMB - 1) // 2
HY_ORDER = 64
HY_TARGET = 1e-2
HY_FAST = 0.3
HY_SLOW = 1.5
C_HEADS = 6
C_Q_LORA = 256
C_KV_LORA = 128
C_NOPE = 64
C_ROPE = 32
C_V = 64
A_WIDTH = A_HEADS * HEAD_DIM
C_WIDTH = C_HEADS * C_V
A_COLS = (A_HEADS + 2 * A_KV_HEADS) * HEAD_DIM
B_COLS = 3 * HY_WIDTH
C_COLS = C_Q_LORA + C_KV_LORA + C_ROPE

LANES = 128
VMEM_LIMIT = 56 * 1024 * 1024

P_COLS = A_COLS + B_COLS + C_Q_LORA + C_KV_LORA + LANES
OFF_K = A_WIDTH
OFF_V = A_WIDTH + A_KV_HEADS * HEAD_DIM
OFF_B = A_COLS
OFF_CQ = A_COLS + B_COLS
OFF_CKV = OFF_CQ + C_Q_LORA
OFF_KR = OFF_CKV + C_KV_LORA
A_HEAD_ORDER = (0, 3, 1, 4, 2, 5)
CQ_PAD = C_HEADS * LANES


def _cparams(sem):
    return pltpu.CompilerParams(dimension_semantics=sem, vmem_limit_bytes=VMEM_LIMIT)


def _layer_norm(x):
    mu = jnp.mean(x, -1, keepdims=True)
    xc = x - mu
    var = jnp.mean(xc * xc, -1, keepdims=True)
    return xc * lax.rsqrt(var + EPS)


def _silu(x):
    return x * jax.nn.sigmoid(x)


def _dot(a, b):
    return jnp.dot(a, b, preferred_element_type=F32)


def _dot_nt(a, b):
    return lax.dot_general(a, b, (((1,), (1,)), ((), ())), preferred_element_type=F32)


def _dot_hp(a, b):
    return jnp.dot(a, b, preferred_element_type=F32, precision=lax.Precision.HIGHEST)


def _ada_kernel(c_ref, w_ref, b_ref, o_ref):
    a = _silu(c_ref[...]).astype(BF16)
    o_ref[0] = _dot(a, w_ref[0].astype(BF16)) + b_ref[0]


def _ada_mod(c_all, ada_w, ada_b):
    depth, d, n = ada_w.shape
    rows = c_all.shape[0]
    tn = n // 8
    return pl.pallas_call(
        _ada_kernel,
        grid=(depth, n // tn),
        in_specs=[
            pl.BlockSpec((rows, d), lambda l, j: (0, 0)),
            pl.BlockSpec((1, d, tn), lambda l, j: (l, 0, j)),
            pl.BlockSpec((1, 1, tn), lambda l, j: (l, 0, j)),
        ],
        out_specs=pl.BlockSpec((1, rows, tn), lambda l, j: (l, 0, j)),
        out_shape=jax.ShapeDtypeStruct((depth, rows, n), F32),
        compiler_params=_cparams(("arbitrary", "arbitrary")),
        name="ada_mod",
    )(c_all, ada_w, ada_b.reshape(depth, 1, n))


def _ffn_kernel(x_ref, mod_ref, wg_ref, wu_ref, wd_ref, g_ref, b_ref, o_ref, *, ck):
    x = x_ref[...]
    mod = mod_ref[0, 0]
    h = (_layer_norm(x) * (1.0 + mod[1:2]) + mod[0:1]).astype(BF16)
    acc = None
    for k in range(wg_ref.shape[1] // ck):
        g = _dot(h, wg_ref[:, k * ck:(k + 1) * ck])
        u = _dot(h, wu_ref[:, k * ck:(k + 1) * ck])
        a = (_silu(g) * u).astype(BF16)
        part = _dot(a, wd_ref[k * ck:(k + 1) * ck, :])
        acc = part if acc is None else acc + part
    y = ALPHA * x + (0.5 * mod[2:3]) * acc
    o_ref[...] = _layer_norm(y) * g_ref[...] + b_ref[...]


def _ffn_sublayer(x2, mods4, mod_group, batch_of_tile, wg, wu, wd, g, b, tm):
    rows, d = x2.shape
    f = wg.shape[1]
    const = lambda i: (0, 0)
    return pl.pallas_call(
        functools.partial(_ffn_kernel, ck=f // 2),
        grid=(rows // tm,),
        in_specs=[
            pl.BlockSpec((tm, d), lambda i: (i, 0)),
            pl.BlockSpec((1, 1, 3, d), lambda i: (batch_of_tile(i), mod_group, 0, 0)),
            pl.BlockSpec((d, f), const, pipeline_mode=pl.Buffered(1)),
            pl.BlockSpec((d, f), const, pipeline_mode=pl.Buffered(1)),
            pl.BlockSpec((f, d), const, pipeline_mode=pl.Buffered(1)),
            pl.BlockSpec((1, d), const),
            pl.BlockSpec((1, d), const),
        ],
        out_specs=pl.BlockSpec((tm, d), lambda i: (i, 0)),
        out_shape=jax.ShapeDtypeStruct((rows, d), F32),
        compiler_params=_cparams(("arbitrary",)),
        name="ffn_sublayer",
    )(x2, mods4, wg, wu, wd, g.reshape(1, d), b.reshape(1, d))


def _rope(x, cos, sin_signed, half):
    lane = lax.broadcasted_iota(jnp.int32, x.shape, 1)
    first = (lane % (2 * half)) < half
    partner = jnp.where(first, pltpu.roll(x, LANES - half, 1), pltpu.roll(x, half, 1))
    return x * cos + partner * sin_signed


def _half_tile_mean_sq(x, e_ref):
    sq = x * x
    hi = sq.astype(BF16)
    lo = (sq - hi.astype(F32)).astype(BF16)
    e = e_ref[...]
    return (_dot(hi, e) + _dot(lo, e)) * (1.0 / HEAD_DIM)


def _mix_in_kernel(x_ref, mod_ref, w_ref, e_ref, gq_ref, gk_ref, cq_g_ref, ckv_g_ref, wuq_ref, wuk_ref, wuv_ref,
                   *rest, use_rope):
    if use_rope:
        cos_a_ref, sin_a_ref, cos_c_ref, sin_c_ref = rest[:4]
        rest = rest[4:]
    qa_ref, ka_ref, va_ref, pb_ref, qc_ref, kc_ref, vc_ref = rest

    mod = mod_ref[0, 0]
    h = (_layer_norm(x_ref[...]) * (1.0 + mod[1:2]) + mod[0:1]).astype(BF16)
    p = _dot(h, w_ref[...])

    def rope_a(t):
        return _rope(t, cos_a_ref[...], sin_a_ref[...], HEAD_DIM // 2) if use_rope else t

    def rope_c(t):
        return _rope(t, cos_c_ref[...], sin_c_ref[...], C_ROPE // 2) if use_rope else t

    for c in range(A_WIDTH // LANES):
        t = p[:, c * LANES:(c + 1) * LANES]
        t = t * lax.rsqrt(_half_tile_mean_sq(t, e_ref) + EPS) * gq_ref[...]
        qa_ref[:, c * LANES:(c + 1) * LANES] = (rope_a(t) * (LOG2E * HEAD_DIM ** -0.5)).astype(BF16)
    t = p[:, OFF_K:OFF_K + LANES]
    t = t * lax.rsqrt(_half_tile_mean_sq(t, e_ref) + EPS) * gk_ref[...]
    ka_ref[...] = rope_a(t).astype(BF16)
    v = p[:, OFF_V:OFF_V + LANES]
    lane = lax.broadcasted_iota(jnp.int32, v.shape, 1)
    low = lane < HEAD_DIM
    va_ref[:, 0:LANES] = jnp.where(low, v, 1.0).astype(BF16)
    va_ref[:, LANES:2 * LANES] = jnp.where(low, 1.0, v).astype(BF16)

    pb_ref[...] = p[:, OFF_B:OFF_B + B_COLS]

    c_q = p[:, OFF_CQ:OFF_CQ + C_Q_LORA]
    c_q = c_q * lax.rsqrt(jnp.mean(c_q * c_q, -1, keepdims=True) + EPS) * cq_g_ref[...]
    qq = _dot(c_q.astype(BF16), wuq_ref[...])
    c_kv = p[:, OFF_CKV:OFF_CKV + C_KV_LORA]
    c_kv = (c_kv * lax.rsqrt(jnp.mean(c_kv * c_kv, -1, keepdims=True) + EPS) * ckv_g_ref[...]).astype(BF16)
    kn = _dot(c_kv, wuk_ref[...])
    vv = _dot(c_kv, wuv_ref[...])
    k_r = rope_c(p[:, OFF_KR:OFF_KR + LANES])
    scale_c = LOG2E * (C_NOPE + C_ROPE) ** -0.5
    for hd in range(C_HEADS):
        sl = slice(hd * LANES, (hd + 1) * LANES)
        qc_ref[:, sl] = (rope_c(qq[:, sl]) * scale_c).astype(BF16)
        kc_ref[:, sl] = (kn[:, sl] + k_r).astype(BF16)
        ones_low = (hd % 2) == 1
        vc_ref[:, sl] = jnp.where(low != ones_low, vv[:, sl], 1.0).astype(BF16)


def _mix_in(x2, mods4, batch_of_tile, wts, rope_tabs, tiles_per_seq, tm):
    rows, d = x2.shape
    const = lambda i: (0, 0)
    use_rope = rope_tabs is not None
    in_specs = [
        pl.BlockSpec((tm, d), lambda i: (i, 0)),
        pl.BlockSpec((1, 1, 3, d), lambda i: (batch_of_tile(i), 1, 0, 0)),
        pl.BlockSpec((d, P_COLS), const, pipeline_mode=pl.Buffered(1)),
        pl.BlockSpec((LANES, LANES), const),
        pl.BlockSpec((1, LANES), const),
        pl.BlockSpec((1, LANES), const),
        pl.BlockSpec((1, C_Q_LORA), const),
        pl.BlockSpec((1, C_KV_LORA), const),
        pl.BlockSpec((C_Q_LORA, CQ_PAD), const),
        pl.BlockSpec((C_KV_LORA, CQ_PAD), const),
        pl.BlockSpec((C_KV_LORA, CQ_PAD), const),
    ]
    args = [x2, mods4, wts["w_in"], wts["e"], wts["gq"], wts["gk"], wts["cq_g"], wts["ckv_g"],
            wts["w_uq"], wts["w_uk"], wts["w_uv"]]
    if use_rope:
        in_specs += [pl.BlockSpec((tm, LANES), lambda i: (i % tiles_per_seq, 0))] * 4
        args += list(rope_tabs)
    row_spec = lambda w: pl.BlockSpec((tm, w), lambda i: (i, 0))
    widths = (A_WIDTH, LANES, 2 * LANES, B_COLS, CQ_PAD, CQ_PAD, CQ_PAD)
    dtypes = (BF16, BF16, BF16, F32, BF16, BF16, BF16)
    return pl.pallas_call(
        functools.partial(_mix_in_kernel, use_rope=use_rope),
        grid=(rows // tm,),
        in_specs=in_specs,
        out_specs=[row_spec(w) for w in widths],
        out_shape=[jax.ShapeDtypeStruct((rows, w), dt) for w, dt in zip(widths, dtypes)],
        compiler_params=_cparams(("arbitrary",)),
        name="mix_in_rope" if use_rope else "mix_in",
    )(*args)


ATTN_TILES_A = (256, 512)
ATTN_TILES_C = (512, 512)
LOG2E = math.log2(math.e)


def _key_chunks(kv_refs, tk):
    chunks = []
    for s in range(len(kv_refs) // 2):
        n_rows = kv_refs[2 * s].shape[1]
        size = min(tk, n_rows)
        chunks += [(s, off, size) for off in range(0, n_rows, size)]
    return chunks


def _flash(q_get, n_chain, chunks, k_get, v_get, scratch):
    s_refs = [scratch[2 * c:2 * c + 2] for c in range(n_chain)]
    m_refs = scratch[2 * n_chain:3 * n_chain]
    acc_refs = scratch[3 * n_chain:4 * n_chain]

    def scores(i):
        size = chunks[i][2]
        for c in range(n_chain):
            s_refs[c][i % 2][:, 0:size] = _dot_nt(q_get(c), k_get(c, chunks[i]))

    def softmax_pv(i):
        size = chunks[i][2]
        for c in range(n_chain):
            slabs = [s_refs[c][i % 2][:, j * LANES:(j + 1) * LANES] for j in range(size // LANES)]
            m_new = functools.reduce(jnp.maximum, slabs)
            m_new = jnp.broadcast_to(jnp.max(m_new, axis=-1, keepdims=True), m_new.shape)
            if i > 0:
                m_old = m_refs[c][...]
                m_new = jnp.maximum(m_old, m_new)
            p = jnp.concatenate([jnp.exp2(sl - m_new).astype(BF16) for sl in slabs], axis=1)
            pv = _dot(p, v_get(c, chunks[i]))
            acc_refs[c][...] = pv if i == 0 else jnp.exp2(m_old - m_new) * acc_refs[c][...] + pv
            m_refs[c][...] = m_new

    scores(0)
    for i in range(len(chunks)):
        if i + 1 < len(chunks):
            scores(i + 1)
        softmax_pv(i)
    return acc_refs


def _attn_scratch(n_chain, m_rows, tk):
    return ([pltpu.VMEM((m_rows, tk), F32)] * (2 * n_chain) + [pltpu.VMEM((m_rows, LANES), F32)] * (2 * n_chain))


def _merge_halves(acc_low, acc_high):
    lane = lax.broadcasted_iota(jnp.int32, acc_low.shape, 1)
    low = lane < HEAD_DIM
    num = jnp.where(low, acc_low, acc_high)
    den = pltpu.roll(jnp.where(low, acc_high, acc_low), HEAD_DIM, 1)
    return num / den


def _gqa_kernel(q_ref, *refs, n_src, tk):
    kv_refs, o_ref = refs[:2 * n_src], refs[2 * n_src]
    qs_ref, scratch = refs[2 * n_src + 1], refs[2 * n_src + 2:]
    tq = q_ref.shape[1]
    n_tiles = A_WIDTH // LANES
    low = lax.broadcasted_iota(jnp.int32, (tq, LANES), 1) < HEAD_DIM
    zero = jnp.zeros((tq, LANES), BF16)
    for g in range(A_KV_HEADS):
        for c in range(n_tiles):
            qs_ref[g, c * tq:(c + 1) * tq, :] = jnp.where(low == (g == 0), q_ref[0, :, c * LANES:(c + 1) * LANES], zero)
    q_get = lambda g: qs_ref[g]
    k_get = lambda g, ch: kv_refs[2 * ch[0]][0, ch[1]:ch[1] + ch[2], :]
    v_get = lambda g, ch: kv_refs[2 * ch[0] + 1][0, ch[1]:ch[1] + ch[2], g * LANES:(g + 1) * LANES]
    acc = _flash(q_get, A_KV_HEADS, _key_chunks(kv_refs, tk), k_get, v_get, scratch)
    for c in range(n_tiles):
        rows = slice(c * tq, (c + 1) * tq)
        o_ref[0, :, c * LANES:(c + 1) * LANES] = _merge_halves(acc[0][rows, :], acc[1][rows, :]).astype(BF16)


def _gqa_attention(q, kv_list, tq, tk):
    b, lq, _ = q.shape
    tq, tk = min(tq, lq), min(tk, max(k.shape[1] for k, _ in kv_list))
    in_specs = [pl.BlockSpec((1, tq, A_WIDTH), lambda i, j: (i, j, 0))]
    args = [q]
    for k, v in kv_list:
        in_specs.append(pl.BlockSpec((1, k.shape[1], LANES), lambda i, j: (i, 0, 0)))
        in_specs.append(pl.BlockSpec((1, v.shape[1], 2 * LANES), lambda i, j: (i, 0, 0)))
        args += [k, v]
    return pl.pallas_call(
        functools.partial(_gqa_kernel, n_src=len(kv_list), tk=tk),
        grid=(b, lq // tq),
        in_specs=in_specs,
        out_specs=pl.BlockSpec((1, tq, A_WIDTH), lambda i, j: (i, j, 0)),
        out_shape=jax.ShapeDtypeStruct((b, lq, A_WIDTH), BF16),
        scratch_shapes=[pltpu.VMEM((A_KV_HEADS, (A_HEADS // A_KV_HEADS) * tq, LANES), BF16)]
        + _attn_scratch(A_KV_HEADS, (A_HEADS // A_KV_HEADS) * tq, tk),
        compiler_params=_cparams(("arbitrary", "arbitrary")),
        name="gqa_attention",
    )(*args)


def _mla_kernel(q_ref, *refs, n_src, tk):
    kv_refs, o_ref = refs[:2 * n_src], refs[2 * n_src]
    scratch = refs[2 * n_src + 1:]
    q_get = lambda e: q_ref[0, :, e * LANES:(e + 1) * LANES]
    k_get = lambda e, ch: kv_refs[2 * ch[0]][0, ch[1]:ch[1] + ch[2], e * LANES:(e + 1) * LANES]
    v_get = lambda e, ch: kv_refs[2 * ch[0] + 1][0, ch[1]:ch[1] + ch[2], e * LANES:(e + 1) * LANES]
    acc = _flash(q_get, 2, _key_chunks(kv_refs, tk), k_get, v_get, scratch)
    o_ref[0] = _merge_halves(acc[0][...], acc[1][...]).astype(BF16)


def _mla_attention(q, kv_list, tq, tk):
    b, lq, _ = q.shape
    tq, tk = min(tq, lq), min(tk, max(k.shape[1] for k, _ in kv_list))
    in_specs = [pl.BlockSpec((1, tq, 2 * LANES), lambda i, hp, j: (i, j, hp))]
    args = [q]
    for k, v in kv_list:
        in_specs.append(pl.BlockSpec((1, k.shape[1], 2 * LANES), lambda i, hp, j: (i, 0, hp)))
        in_specs.append(pl.BlockSpec((1, v.shape[1], 2 * LANES), lambda i, hp, j: (i, 0, hp)))
        args += [k, v]
    return pl.pallas_call(
        functools.partial(_mla_kernel, n_src=len(kv_list), tk=tk),
        grid=(b, C_HEADS // 2, lq // tq),
        in_specs=in_specs,
        out_specs=pl.BlockSpec((1, tq, LANES), lambda i, hp, j: (i, j, hp)),
        out_shape=jax.ShapeDtypeStruct((b, lq, C_WIDTH), BF16),
        scratch_shapes=_attn_scratch(2, tq, tk),
        compiler_params=_cparams(("arbitrary", "arbitrary", "arbitrary")),
        name="mla_attention",
    )(*args)


def _fft_dims(seq):
    n = 2 * seq
    n2 = min(LANES, n // 16)
    return n // n2, n2


def _stack_real(m):
    return np.block([[m.real, -m.imag], [m.imag, m.real]]).astype(np.float32)


@functools.lru_cache(maxsize=None)
def _fft_constants(seq):
    n1, n2 = _fft_dims(seq)
    n = n1 * n2
    k1 = np.arange(n1)[:, None]
    f1 = np.exp(-2j * np.pi * k1 * np.arange(n1 // 2)[None, :] / n1)
    f2 = np.exp(-2j * np.pi * np.arange(n2)[:, None] * np.arange(n2)[None, :] / n2)
    tw = np.exp(-2j * np.pi * k1 * np.arange(n2)[None, :] / n)
    g1 = np.conj(f1).T / n
    return dict(
        m1=_stack_real(f1), m2=_stack_real(f2), m2_inv=_stack_real(np.conj(f2)), m1_inv=_stack_real(g1),
        tw_r=tw.real.astype(np.float32)[:, :, None], tw_i=tw.imag.astype(np.float32)[:, :, None])


def _hy_prep_kernel(v_ref, x1_ref, x0_ref, wv_ref, wx1_ref, wx0_ref, bv_ref, bx1_ref, bx0_ref, u_ref, x0c_ref):
    seq = v_ref.shape[1]
    row = lax.broadcasted_iota(jnp.int32, (seq, LANES), 0)
    first, last = row == 0, row == seq - 1

    def conv(p_ref, w_ref, b_ref):
        t = p_ref[0]
        prev = jnp.where(first, 0.0, pltpu.roll(t, 1, 0))
        nxt = jnp.where(last, 0.0, pltpu.roll(t, seq - 1, 0))
        return prev * w_ref[0:1] + t * w_ref[1:2] + nxt * w_ref[2:3] + b_ref[...]

    u_ref[0] = conv(v_ref, wv_ref, bv_ref) * conv(x1_ref, wx1_ref, bx1_ref)
    x0c_ref[0] = conv(x0_ref, wx0_ref, bx0_ref)


def _hy_prep(pb, conv_w, conv_b):
    b, seq, _ = pb.shape
    n_c = HY_WIDTH // LANES
    p_spec = lambda part: pl.BlockSpec((1, seq, LANES), lambda i, c: (i, 0, part * n_c + c))
    w_spec = lambda part: pl.BlockSpec((3, LANES), lambda i, c: (0, part * n_c + c))
    b_spec = lambda part: pl.BlockSpec((1, LANES), lambda i, c: (0, part * n_c + c))
    out_spec = pl.BlockSpec((1, seq, LANES), lambda i, c: (i, 0, c))
    cb = conv_b.reshape(1, B_COLS)
    return pl.pallas_call(
        _hy_prep_kernel,
        grid=(b, n_c),
        in_specs=[p_spec(0), p_spec(1), p_spec(2), w_spec(0), w_spec(1), w_spec(2), b_spec(0), b_spec(1), b_spec(2)],
        out_specs=[out_spec, out_spec],
        out_shape=[jax.ShapeDtypeStruct((b, seq, HY_WIDTH), F32)] * 2,
        compiler_params=_cparams(("arbitrary", "arbitrary")),
        name="hyena_prep",
    )(pb, pb, pb, conv_w, conv_w, conv_w, cb, cb, cb)


def _hy_stage1_kernel(u_ref, m1_ref, a_ref):
    _, h, lc = u_ref.shape
    z = u_ref[...].reshape(2 * h, lc).astype(BF16)
    a_ref[0] = _dot(m1_ref[...].astype(BF16), z).reshape(2, 2 * h, lc)


def _hy_stage1(u, seq):
    b, _, ch = u.shape
    n1, n2 = _fft_dims(seq)
    cols = n2 * ch
    lc = min(cols, 4096)
    m1 = jnp.asarray(_fft_constants(seq)["m1"])
    return pl.pallas_call(
        _hy_stage1_kernel,
        grid=(b // 2, cols // lc),
        in_specs=[pl.BlockSpec((2, n1 // 2, lc), lambda j, c: (j, 0, c)),
                  pl.BlockSpec((2 * n1, n1), lambda j, c: (0, 0))],
        out_specs=pl.BlockSpec((1, 2, n1, lc), lambda j, c: (j, 0, 0, c)),
        out_shape=jax.ShapeDtypeStruct((b // 2, 2, n1, cols), F32),
        compiler_params=_cparams(("arbitrary", "arbitrary")),
        name="hyena_stage1",
    )(u.reshape(b, n1 // 2, cols), m1)


def _twiddle(ar, ai, tr, ti):
    return ar * tr - ai * ti, ar * ti + ai * tr


def _hy_filter_spec_kernel(a_ref, twr_ref, twi_ref, m2_ref, inv_ref, h_ref):
    n2 = a_ref.shape[3]
    m2 = m2_ref[...].astype(BF16)
    tr, ti = twr_ref[0], twi_ref[0]

    def fwd(pair):
        br, bi = _twiddle(a_ref[pair, 0, 0], a_ref[pair, 1, 0], tr, ti)
        return _dot(m2, jnp.concatenate([br, bi], axis=0).astype(BF16))

    zf, zb = fwd(0), fwd(1)
    h_ref[0, 0:n2] = (zf[0:n2] + zb[0:n2]) * inv_ref[...]
    h_ref[0, n2:2 * n2] = (zf[n2:2 * n2] - zb[n2:2 * n2]) * inv_ref[...]


def _hy_filter_spec(a, inv_norm, seq):
    n1, n2 = _fft_dims(seq)
    cst = _fft_constants(seq)
    a5 = a.reshape(2, 2, n1, n2, HY_WIDTH)
    return pl.pallas_call(
        _hy_filter_spec_kernel,
        grid=(n1,),
        in_specs=[pl.BlockSpec((2, 2, 1, n2, HY_WIDTH), lambda k: (0, 0, k, 0, 0)),
                  pl.BlockSpec((1, n2, 1), lambda k: (k, 0, 0)),
                  pl.BlockSpec((1, n2, 1), lambda k: (k, 0, 0)),
                  pl.BlockSpec((2 * n2, 2 * n2), lambda k: (0, 0)),
                  pl.BlockSpec((1, HY_WIDTH), lambda k: (0, 0))],
        out_specs=pl.BlockSpec((1, 2 * n2, HY_WIDTH), lambda k: (k, 0, 0)),
        out_shape=jax.ShapeDtypeStruct((n1, 2 * n2, HY_WIDTH), F32),
        compiler_params=_cparams(("arbitrary",)),
        name="hyena_filter_spectrum",
    )(a5, jnp.asarray(cst["tw_r"]), jnp.asarray(cst["tw_i"]), jnp.asarray(cst["m2"]), inv_norm)


def _hy_mid_kernel(a_ref, twr_ref, twi_ref, m2_ref, m2i_ref, h_ref, q_ref):
    n2 = a_ref.shape[3]
    tr, ti = twr_ref[0], twi_ref[0]
    br, bi = _twiddle(a_ref[0, 0, 0], a_ref[0, 1, 0], tr, ti)
    z = _dot(m2_ref[...].astype(BF16), jnp.concatenate([br, bi], axis=0).astype(BF16))
    zr, zi = z[0:n2], z[n2:2 * n2]
    hr, hi = h_ref[0, 0:n2], h_ref[0, n2:2 * n2]
    yr, yi = zr * hr - zi * hi, zr * hi + zi * hr
    p = _dot(m2i_ref[...].astype(BF16), jnp.concatenate([yr, yi], axis=0).astype(BF16))
    qr, qi = _twiddle(p[0:n2], p[n2:2 * n2], tr, -ti)
    q_ref[0, 0, 0] = qr
    q_ref[0, 1, 0] = qi


def _hy_mid(a, h_spec, seq):
    pairs = a.shape[0]
    n1, n2 = _fft_dims(seq)
    cst = _fft_constants(seq)
    a5 = a.reshape(pairs, 2, n1, n2, HY_WIDTH)
    blk = pl.BlockSpec((1, 2, 1, n2, HY_WIDTH), lambda k, j: (j, 0, k, 0, 0))
    q = pl.pallas_call(
        _hy_mid_kernel,
        grid=(n1, pairs),
        in_specs=[blk,
                  pl.BlockSpec((1, n2, 1), lambda k, j: (k, 0, 0)),
                  pl.BlockSpec((1, n2, 1), lambda k, j: (k, 0, 0)),
                  pl.BlockSpec((2 * n2, 2 * n2), lambda k, j: (0, 0)),
                  pl.BlockSpec((2 * n2, 2 * n2), lambda k, j: (0, 0)),
                  pl.BlockSpec((1, 2 * n2, HY_WIDTH), lambda k, j: (k, 0, 0))],
        out_specs=blk,
        out_shape=jax.ShapeDtypeStruct((pairs, 2, n1, n2, HY_WIDTH), F32),
        compiler_params=_cparams(("arbitrary", "arbitrary")),
        name="hyena_mid",
    )(a5, jnp.asarray(cst["tw_r"]), jnp.asarray(cst["tw_i"]), jnp.asarray(cst["m2"]), jnp.asarray(cst["m2_inv"]),
      h_spec)
    return q.reshape(pairs, 2, n1, n2 * HY_WIDTH)


def _hy_out_kernel(q_ref, m1i_ref, u_ref, x0_ref, d_ref, o_ref):
    _, _, n1, lc = q_ref.shape
    q = q_ref[0].reshape(2 * n1, lc).astype(BF16)
    y = _dot(m1i_ref[...].astype(BF16), q).reshape(2, n1 // 2, lc)
    o_ref[...] = ((y + u_ref[...] * d_ref[...]) * x0_ref[...]).astype(o_ref.dtype)


def _hy_out(q, u, x0c, d_skip, seq):
    b, _, ch = u.shape
    n1, n2 = _fft_dims(seq)
    cols = n2 * ch
    lc = min(cols, 4096)
    m1i = jnp.asarray(_fft_constants(seq)["m1_inv"])
    blk = pl.BlockSpec((2, n1 // 2, lc), lambda j, c: (j, 0, c))
    out = pl.pallas_call(
        _hy_out_kernel,
        grid=(b // 2, cols // lc),
        in_specs=[pl.BlockSpec((1, 2, n1, lc), lambda j, c: (j, 0, 0, c)),
                  pl.BlockSpec((n1, 2 * n1), lambda j, c: (0, 0)),
                  blk, blk,
                  pl.BlockSpec((1, 1, lc), lambda j, c: (0, 0, c))],
        out_specs=blk,
        out_shape=jax.ShapeDtypeStruct((b, n1 // 2, cols), BF16),
        compiler_params=_cparams(("arbitrary", "arbitrary")),
        name="hyena_out",
    )(q, m1i, u.reshape(b, n1 // 2, cols), x0c.reshape(b, n1 // 2, cols),
      jnp.tile(d_skip, n2).reshape(1, 1, cols))
    return out.reshape(b, seq, ch)


def _hy_filter_kernel(z_ref, t_ref, w1_ref, b1_ref, w2_ref, b2_ref, w3_ref, b3_ref, w4_ref, fr_ref, dl_ref,
                      taps_ref, inv_ref):
    fr = fr_ref[...]
    hdn = jnp.sin(fr * (_dot_hp(z_ref[...], w1_ref[...]) + b1_ref[...]))
    hdn = jnp.sin(fr * (_dot_hp(hdn, w2_ref[...]) + b2_ref[...]))
    hdn = jnp.sin(fr * (_dot_hp(hdn, w3_ref[...]) + b3_ref[...]))
    h = _dot_hp(hdn, w4_ref[...]) * jnp.exp(-t_ref[...] * dl_ref[...])
    h_fwd, h_bwd = h[:, :HY_WIDTH], h[:, HY_WIDTH:]
    row = lax.broadcasted_iota(jnp.int32, h_bwd.shape, 0)
    h_bwd = jnp.where(row == 0, 0.0, h_bwd)
    norm = jnp.sum(jnp.abs(h_fwd), 0, keepdims=True) + jnp.sum(jnp.abs(h_bwd), 0, keepdims=True)
    inv_ref[...] = 1.0 / norm
    zeros = jnp.zeros_like(h_fwd)
    taps_ref[0] = h_fwd
    taps_ref[1] = zeros
    taps_ref[2] = h_bwd
    taps_ref[3] = zeros


def _hy_filter(seq, w1, b1, w2, b2, w3, b3, w4, freq):
    t = jnp.linspace(0.0, 1.0, seq, dtype=F32)[:, None]
    w = 2.0 * math.pi * jnp.arange(seq, dtype=F32)[:, None] / seq
    f = jnp.linspace(1e-4, HY_BANDS - 1, HY_BANDS, dtype=F32)[None, :]
    z = jnp.concatenate([t, jnp.cos(f * w), -jnp.sin(f * w)], -1)
    z = jnp.pad(z, ((0, 0), (0, HY_ORDER - HY_EMB)))
    w1p = jnp.pad(w1, ((0, HY_ORDER - HY_EMB), (0, 0)))
    deltas = jnp.abs(jnp.linspace(math.log(HY_TARGET) / HY_SLOW, math.log(HY_TARGET) / HY_FAST, HY_WIDTH, dtype=F32))
    row = lambda v: v.reshape(1, -1)
    return pl.pallas_call(
        _hy_filter_kernel,
        out_shape=[jax.ShapeDtypeStruct((4, seq, HY_WIDTH), F32), jax.ShapeDtypeStruct((1, HY_WIDTH), F32)],
        compiler_params=pltpu.CompilerParams(vmem_limit_bytes=VMEM_LIMIT),
        name="hyena_filter",
    )(z, t, w1p, row(b1), w2, row(b2), w3, row(b3), w4, row(freq), row(jnp.tile(deltas, 2)))


def _hyena(pb, hy, seq):
    conv_w, conv_b, w1, b1, w2, b2, w3, b3, w4, freq, d_skip = hy
    taps, inv_norm = _hy_filter(seq, w1, b1, w2, b2, w3, b3, w4, freq)
    h_spec = _hy_filter_spec(_hy_stage1(taps, seq), inv_norm, seq)
    u, x0c = _hy_prep(pb, conv_w, conv_b)
    q = _hy_mid(_hy_stage1(u, seq), h_spec, seq)
    return _hy_out(q, u, x0c, d_skip, seq)


def _mix_out_kernel(x_ref, mod_ref, oa_ref, ob_ref, oc_ref, wa_ref, wb_ref, wc_ref, g_ref, b_ref, o_ref):
    mix = _dot(oa_ref[...], wa_ref[...]) + _dot(ob_ref[...], wb_ref[...]) + _dot(oc_ref[...], wc_ref[...])
    y = ALPHA * x_ref[...] + mod_ref[0, 0][2:3] * mix
    o_ref[...] = _layer_norm(y) * g_ref[...] + b_ref[...]


def _mix_out(x2, mods4, batch_of_tile, oa, ob, oc, wa, wb, wc, g, b, tm):
    rows, d = x2.shape
    const = lambda i: (0, 0)
    row_spec = lambda w: pl.BlockSpec((tm, w), lambda i: (i, 0))
    return pl.pallas_call(
        _mix_out_kernel,
        grid=(rows // tm,),
        in_specs=[row_spec(d),
                  pl.BlockSpec((1, 1, 3, d), lambda i: (batch_of_tile(i), 1, 0, 0)),
                  row_spec(A_WIDTH), row_spec(HY_WIDTH), row_spec(C_WIDTH),
                  pl.BlockSpec((A_WIDTH, d), const), pl.BlockSpec((HY_WIDTH, d), const),
                  pl.BlockSpec((C_WIDTH, d), const),
                  pl.BlockSpec((1, d), const), pl.BlockSpec((1, d), const)],
        out_specs=row_spec(d),
        out_shape=jax.ShapeDtypeStruct((rows, d), F32),
        compiler_params=_cparams(("arbitrary",)),
        name="mix_out",
    )(x2, mods4, oa, ob, oc, wa, wb, wc, g.reshape(1, d), b.reshape(1, d))


def _take_cols(w, idx):
    idx = np.asarray(idx)
    return jnp.where(jnp.asarray(idx >= 0)[None, :], w[:, np.maximum(idx, 0)], 0.0)


def _mix_weights(w_in, w_out, a_qn, a_kn, q_g, kv_g, w_uq, w_ukv):
    col = np.arange
    q_cols = np.concatenate([col(h * HEAD_DIM, (h + 1) * HEAD_DIM) for h in A_HEAD_ORDER])
    pad = lambda n: -np.ones(n, np.int64)
    in_idx = np.concatenate([q_cols, col(A_WIDTH, OFF_KR), pad(HEAD_DIM), col(OFF_KR, OFF_KR + C_ROPE),
                             pad(LANES - HEAD_DIM - C_ROPE)])
    dq = C_NOPE + C_ROPE
    uq_idx = np.concatenate([np.concatenate([col(h * dq, (h + 1) * dq), pad(LANES - dq)]) for h in range(C_HEADS)])
    dkv = C_NOPE + C_V
    uk_idx = np.concatenate([np.concatenate([col(h * dkv, h * dkv + C_NOPE), pad(LANES - C_NOPE)])
                             for h in range(C_HEADS)])
    v_cols = lambda h: col(h * dkv + C_NOPE, (h + 1) * dkv)
    uv_idx = np.concatenate([np.concatenate([v_cols(h), pad(LANES - C_V)] if h % 2 == 0 else
                                            [pad(LANES - C_V), v_cols(h)]) for h in range(C_HEADS)])
    half = np.arange(LANES) // HEAD_DIM
    e = (half[:, None] == half[None, :]).astype(np.float32)
    tile2 = lambda g: jnp.tile(g, LANES // HEAD_DIM).reshape(1, LANES)
    return dict(
        w_in=_take_cols(w_in, in_idx).astype(BF16),
        e=jnp.asarray(e, BF16),
        gq=tile2(a_qn), gk=tile2(a_kn),
        cq_g=q_g.reshape(1, -1), ckv_g=kv_g.reshape(1, -1),
        w_uq=_take_cols(w_uq, uq_idx).astype(BF16),
        w_uk=_take_cols(w_ukv, uk_idx).astype(BF16),
        w_uv=_take_cols(w_ukv, uv_idx).astype(BF16),
        wa=w_out[q_cols].astype(BF16),
        wb=w_out[A_WIDTH:A_WIDTH + HY_WIDTH].astype(BF16),
        wc=w_out[A_WIDTH + HY_WIDTH:].astype(BF16),
    )


def _rope_tables(seq):
    rows = seq // GRID_W
    row = jnp.repeat(jnp.arange(rows, dtype=F32), GRID_W)
    colv = jnp.tile(jnp.arange(GRID_W, dtype=F32), rows)

    def cos_sin(rot_dim):
        n_freq = rot_dim // 4
        inv = ROPE_THETA ** (-jnp.arange(n_freq, dtype=F32) / n_freq)
        ang = jnp.concatenate([row[:, None] * inv, colv[:, None] * inv], -1)
        return jnp.cos(ang), jnp.sin(ang)

    cos, sin = cos_sin(HEAD_DIM)
    cos_a = jnp.tile(cos, (1, 2 * LANES // HEAD_DIM))
    sin_a = jnp.tile(jnp.concatenate([-sin, sin], -1), (1, LANES // HEAD_DIM))
    cos, sin = cos_sin(C_ROPE)
    ones = jnp.ones((seq, C_NOPE), F32)
    tail = LANES - C_NOPE - C_ROPE
    cos_c = jnp.concatenate([ones, cos, cos, jnp.ones((seq, tail), F32)], -1)
    sin_c = jnp.concatenate([0.0 * ones, -sin, sin, jnp.zeros((seq, tail), F32)], -1)
    return cos_a, sin_a, cos_c, sin_c


def kernel(x, c, ctx, c_ctx, ada_w, ada_b, ffn1_w_gu, ffn1_w_down, ffn2_w_gu, ffn2_w_down, ln_g, ln_b, w_in, w_out,
           a_q_norm, a_k_norm, hy_conv_w, hy_conv_b, hy_f_w1, hy_f_b1, hy_f_w2, hy_f_b2, hy_f_w3, hy_f_b3, hy_f_w4,
           hy_f_freq, hy_bias, mla_q_norm, mla_kv_norm, mla_w_uq, mla_w_ukv):
    b, seq, d = x.shape
    n_ctx = ctx.shape[1]
    depth = ada_w.shape[0]
    assert b % 2 == 0 and seq % GRID_W == 0

    tm = min(512, seq)
    tm_ctx = min(512, b * n_ctx)
    tiles_per_seq = seq // tm
    lat_batch = lambda i: i // tiles_per_seq
    ctx_batch = lambda i: b

    mod_rows = ((b + 1 + 7) // 8) * 8
    c_all = jnp.concatenate([c, c_ctx[None], jnp.zeros((mod_rows - b - 1, d), F32)], 0)
    mods = _ada_mod(c_all, ada_w, ada_b).reshape(depth, mod_rows, N_MOD // 3, 3, d)

    rope_tabs = _rope_tables(seq)
    x2 = x.reshape(b * seq, d)
    ctx2 = ctx.reshape(b * n_ctx, d)

    for l in range(depth):
        need_ctx = l < depth - 1
        m4 = mods[l]
        ffn1 = (ffn1_w_gu[l, :, :D_FF].astype(BF16), ffn1_w_gu[l, :, D_FF:].astype(BF16),
                ffn1_w_down[l].astype(BF16))
        ffn2 = (ffn2_w_gu[l, :, :D_FF].astype(BF16), ffn2_w_gu[l, :, D_FF:].astype(BF16),
                ffn2_w_down[l].astype(BF16))
        wts = _mix_weights(w_in[l], w_out[l], a_q_norm[l], a_k_norm[l], mla_q_norm[l], mla_kv_norm[l],
                           mla_w_uq[l], mla_w_ukv[l])
        hy = (hy_conv_w[l], hy_conv_b[l], hy_f_w1[l], hy_f_b1[l], hy_f_w2[l], hy_f_b2[l], hy_f_w3[l], hy_f_b3[l],
              hy_f_w4[l], hy_f_freq[l], hy_bias[l])

        x2 = _ffn_sublayer(x2, m4, 0, lat_batch, *ffn1, ln_g[l, 0], ln_b[l, 0], tm)
        ctx2 = _ffn_sublayer(ctx2, m4, 0, ctx_batch, *ffn1, ln_g[l, 0], ln_b[l, 0], tm_ctx)

        qa, ka, va, pb, qc, kc, vc = _mix_in(x2, m4, lat_batch, wts, rope_tabs, tiles_per_seq, tm)
        qa_c, ka_c, va_c, pb_c, qc_c, kc_c, vc_c = _mix_in(ctx2, m4, ctx_batch, wts, None, 1, tm_ctx)
        r3 = lambda a, n: a.reshape(b, n, a.shape[-1])
        ka_c, va_c, kc_c, vc_c = (r3(a, n_ctx) for a in (ka_c, va_c, kc_c, vc_c))

        oa = _gqa_attention(r3(qa, seq), [(ka_c, va_c), (r3(ka, seq), r3(va, seq))], *ATTN_TILES_A)
        oc = _mla_attention(r3(qc, seq), [(kc_c, vc_c), (r3(kc, seq), r3(vc, seq))], *ATTN_TILES_C)
        ob = _hyena(r3(pb, seq), hy, seq)
        flat = lambda a: a.reshape(-1, a.shape[-1])
        x_mix = _mix_out(x2, m4, lat_batch, flat(oa), flat(ob), flat(oc), wts["wa"], wts["wb"], wts["wc"],
                         ln_g[l, 1], ln_b[l, 1], tm)
        if need_ctx:
            oa_c = _gqa_attention(r3(qa_c, n_ctx), [(ka_c, va_c)], *ATTN_TILES_A)
            oc_c = _mla_attention(r3(qc_c, n_ctx), [(kc_c, vc_c)], *ATTN_TILES_C)
            ob_c = _hyena(r3(pb_c, n_ctx), hy, n_ctx)
            ctx2 = _mix_out(ctx2, m4, ctx_batch, flat(oa_c), flat(ob_c), flat(oc_c), wts["wa"], wts["wb"],
                            wts["wc"], ln_g[l, 1], ln_b[l, 1], tm_ctx)
            ctx2 = _ffn_sublayer(ctx2, m4, 2, ctx_batch, *ffn2, ln_g[l, 2], ln_b[l, 2], tm_ctx)
        x2 = _ffn_sublayer(x_mix, m4, 2, lat_batch, *ffn2, ln_g[l, 2], ln_b[l, 2], tm)
    return x2.reshape(b, seq, d)
```

```python
import functools
import math

import numpy as np
import jax
import jax.numpy as jnp
from jax import lax
from jax.experimental import pallas as pl
from jax.experimental.pallas import tpu as pltpu

F32 = jnp.float32
BF16 = jnp.bfloat16

D_MODEL = 1024
GRID_W = 64
N_MOD = 9
D_FF = 2816
DEPTH = 2
ALPHA = (2 * DEPTH) ** 0.25
EPS = 1e-6
A_HEADS = 6
A_KV_HEADS = 2
HEAD_DIM = 64
ROPE_THETA = 10000.0
HY_WIDTH = 256
HY_EMB = 33
HY_BANDS = (HY_EMB - 1) // 2
HY_ORDER = 64
HY_TARGET = 1e-2
HY_FAST = 0.3
HY_SLOW = 1.5
C_HEADS = 6
C_Q_LORA = 256
C_KV_LORA = 128
C_NOPE = 64
C_ROPE = 32
C_V = 64
A_WIDTH = A_HEADS * HEAD_DIM
C_WIDTH = C_HEADS * C_V
A_COLS = (A_HEADS + 2 * A_KV_HEADS) * HEAD_DIM
B_COLS = 3 * HY_WIDTH
C_COLS = C_Q_LORA + C_KV_LORA + C_ROPE

LANES = 128
MXU_DIM = 256
VMEM_LIMIT = 56 * 1024 * 1024

P_COLS = A_COLS + B_COLS + C_Q_LORA + C_KV_LORA + LANES
OFF_K = A_WIDTH
OFF_V = A_WIDTH + A_KV_HEADS * HEAD_DIM
OFF_B = A_COLS
OFF_CQ = A_COLS + B_COLS
OFF_CKV = OFF_CQ + C_Q_LORA
OFF_KR = OFF_CKV + C_KV_LORA
A_HEAD_ORDER = (0, 3, 1, 4, 2, 5)
CQ_PAD = C_HEADS * LANES


def _cparams(sem):
    return pltpu.CompilerParams(dimension_semantics=sem, vmem_limit_bytes=VMEM_LIMIT)


def _layer_norm(x):
    mu = jnp.mean(x, -1, keepdims=True)
    xc = x - mu
    var = jnp.mean(xc * xc, -1, keepdims=True)
    return xc * lax.rsqrt(var + EPS)


def _silu(x):
    return x * jax.nn.sigmoid(x)


def _dot(a, b):
    return jnp.dot(a, b, preferred_element_type=F32)


def _dot_nt(a, b):
    return lax.dot_general(a, b, (((1,), (1,)), ((), ())), preferred_element_type=F32)


def _dot_hp(a, b):
    return jnp.dot(a, b, preferred_element_type=F32, precision=lax.Precision.HIGHEST)


def _ada_kernel(c_ref, w_ref, b_ref, o_ref):
    a = _silu(c_ref[...]).astype(BF16)
    o_ref[0] = _dot(a, w_ref[0].astype(BF16)) + b_ref[0]


def _ada_mod(c_all, ada_w, ada_b):
    depth, d, n = ada_w.shape
    rows = c_all.shape[0]
    tn = n // 8
    return pl.pallas_call(
        _ada_kernel,
        grid=(depth, n // tn),
        in_specs=[
            pl.BlockSpec((rows, d), lambda l, j: (0, 0)),
            pl.BlockSpec((1, d, tn), lambda l, j: (l, 0, j)),
            pl.BlockSpec((1, 1, tn), lambda l, j: (l, 0, j)),
        ],
        out_specs=pl.BlockSpec((1, rows, tn), lambda l, j: (l, 0, j)),
        out_shape=jax.ShapeDtypeStruct((depth, rows, n), F32),
        compiler_params=_cparams(("arbitrary", "arbitrary")),
        name="ada_mod",
    )(c_all, ada_w, ada_b.reshape(depth, 1, n))


def _ffn_kernel(x_ref, mod_ref, wg_ref, wu_ref, wd_ref, g_ref, b_ref, o_ref, *, bounds):
    x = x_ref[...]
    mod = mod_ref[0, 0]
    h = (_layer_norm(x) * (1.0 + mod[1:2]) + mod[0:1]).astype(BF16)
    acc = None
    for lo, hi in zip(bounds[:-1], bounds[1:]):
        g = _dot(h, wg_ref[:, lo:hi])
        u = _dot(h, wu_ref[:, lo:hi])
        a = (_silu(g) * u).astype(BF16)
        part = _dot(a, wd_ref[lo:hi, :])
        acc = part if acc is None else acc + part
    y = ALPHA * x + (0.5 * mod[2:3]) * acc
    o_ref[...] = _layer_norm(y) * g_ref[...] + b_ref[...]


def _ffn_sublayer(x2, mods4, mod_group, batch_of_tile, wg, wu, wd, g, b, tm):
    rows, d = x2.shape
    f = wg.shape[1]
    const = lambda i: (0, 0)
    n_tiles = -(-f // MXU_DIM)
    bounds = (0, min(f, ((n_tiles + 1) // 2) * MXU_DIM), f)
    return pl.pallas_call(
        functools.partial(_ffn_kernel, bounds=bounds),
        grid=(rows // tm,),
        in_specs=[
            pl.BlockSpec((tm, d), lambda i: (i, 0)),
            pl.BlockSpec((1, 1, 3, d), lambda i: (batch_of_tile(i), mod_group, 0, 0)),
            pl.BlockSpec((d, f), const, pipeline_mode=pl.Buffered(1)),
            pl.BlockSpec((d, f), const, pipeline_mode=pl.Buffered(1)),
            pl.BlockSpec((f, d), const, pipeline_mode=pl.Buffered(1)),
            pl.BlockSpec((1, d), const),
            pl.BlockSpec((1, d), const),
        ],
        out_specs=pl.BlockSpec((tm, d), lambda i: (i, 0)),
        out_shape=jax.ShapeDtypeStruct((rows, d), F32),
        compiler_params=_cparams(("arbitrary",)),
        name="ffn_sublayer",
    )(x2, mods4, wg, wu, wd, g.reshape(1, d), b.reshape(1, d))


def _rope(x, cos, sin_signed, half):
    lane = lax.broadcasted_iota(jnp.int32, x.shape, 1)
    first = (lane % (2 * half)) < half
    partner = jnp.where(first, pltpu.roll(x, LANES - half, 1), pltpu.roll(x, half, 1))
    return x * cos + partner * sin_signed


def _half_tile_mean_sq(x, e_ref):
    return _dot((x * x).astype(BF16), e_ref[...]) * (1.0 / HEAD_DIM)


def _mix_in_kernel(x_ref, mod_ref, w_ref, e_ref, gq_ref, gk_ref, cq_g_ref, ckv_g_ref, wuq_ref, wuk_ref, wuv_ref,
                   *rest, use_rope):
    if use_rope:
        cos_a_ref, sin_a_ref, cos_c_ref, sin_c_ref = rest[:4]
        rest = rest[4:]
    qa_ref, ka_ref, va_ref, pb_ref, qc_ref, kc_ref, vc_ref = rest

    mod = mod_ref[0, 0]
    h = (_layer_norm(x_ref[...]) * (1.0 + mod[1:2]) + mod[0:1]).astype(BF16)
    p = _dot(h, w_ref[...])

    def rope_a(t):
        return _rope(t, cos_a_ref[...], sin_a_ref[...], HEAD_DIM // 2) if use_rope else t

    def rope_c(t):
        return _rope(t, cos_c_ref[...], sin_c_ref[...], C_ROPE // 2) if use_rope else t

    for c in range(A_WIDTH // LANES):
        t = p[:, c * LANES:(c + 1) * LANES]
        t = t * lax.rsqrt(_half_tile_mean_sq(t, e_ref) + EPS) * gq_ref[...]
        qa_ref[:, c * LANES:(c + 1) * LANES] = (rope_a(t) * (LOG2E * HEAD_DIM ** -0.5)).astype(BF16)
    t = p[:, OFF_K:OFF_K + LANES]
    t = t * lax.rsqrt(_half_tile_mean_sq(t, e_ref) + EPS) * gk_ref[...]
    ka_ref[...] = rope_a(t).astype(BF16)
    v = p[:, OFF_V:OFF_V + LANES]
    lane = lax.broadcasted_iota(jnp.int32, v.shape, 1)
    low = lane < HEAD_DIM
    va_ref[:, 0:LANES] = jnp.where(low, v, 1.0).astype(BF16)
    va_ref[:, LANES:2 * LANES] = jnp.where(low, 1.0, v).astype(BF16)

    pb_ref[...] = p[:, OFF_B:OFF_B + B_COLS]

    c_q = p[:, OFF_CQ:OFF_CQ + C_Q_LORA]
    c_q = c_q * lax.rsqrt(jnp.mean(c_q * c_q, -1, keepdims=True) + EPS) * cq_g_ref[...]
    qq = _dot(c_q.astype(BF16), wuq_ref[...])
    c_kv = p[:, OFF_CKV:OFF_CKV + C_KV_LORA]
    c_kv = (c_kv * lax.rsqrt(jnp.mean(c_kv * c_kv, -1, keepdims=True) + EPS) * ckv_g_ref[...]).astype(BF16)
    kn = _dot(c_kv, wuk_ref[...])
    vv = _dot(c_kv, wuv_ref[...])
    k_r = rope_c(p[:, OFF_KR:OFF_KR + LANES])
    scale_c = LOG2E * (C_NOPE + C_ROPE) ** -0.5
    for hd in range(C_HEADS):
        sl = slice(hd * LANES, (hd + 1) * LANES)
        qc_ref[:, sl] = (rope_c(qq[:, sl]) * scale_c).astype(BF16)
        kc_ref[:, sl] = (kn[:, sl] + k_r).astype(BF16)
        ones_low = (hd % 2) == 1
        vc_ref[:, sl] = jnp.where(low != ones_low, vv[:, sl], 1.0).astype(BF16)


def _mix_in(x2, mods4, batch_of_tile, wts, rope_tabs, tiles_per_seq, tm):
    rows, d = x2.shape
    const = lambda i: (0, 0)
    use_rope = rope_tabs is not None
    in_specs = [
        pl.BlockSpec((tm, d), lambda i: (i, 0)),
        pl.BlockSpec((1, 1, 3, d), lambda i: (batch_of_tile(i), 1, 0, 0)),
        pl.BlockSpec((d, P_COLS), const, pipeline_mode=pl.Buffered(1)),
        pl.BlockSpec((LANES, LANES), const),
        pl.BlockSpec((1, LANES), const),
        pl.BlockSpec((1, LANES), const),
        pl.BlockSpec((1, C_Q_LORA), const),
        pl.BlockSpec((1, C_KV_LORA), const),
        pl.BlockSpec((C_Q_LORA, CQ_PAD), const),
        pl.BlockSpec((C_KV_LORA, CQ_PAD), const),
        pl.BlockSpec((C_KV_LORA, CQ_PAD), const),
    ]
    args = [x2, mods4, wts["w_in"], wts["e"], wts["gq"], wts["gk"], wts["cq_g"], wts["ckv_g"],
            wts["w_uq"], wts["w_uk"], wts["w_uv"]]
    if use_rope:
        in_specs += [pl.BlockSpec((tm, LANES), lambda i: (i % tiles_per_seq, 0))] * 4
        args += list(rope_tabs)
    row_spec = lambda w: pl.BlockSpec((tm, w), lambda i: (i, 0))
    widths = (A_WIDTH, LANES, 2 * LANES, B_COLS, CQ_PAD, CQ_PAD, CQ_PAD)
    dtypes = (BF16, BF16, BF16, F32, BF16, BF16, BF16)
    return pl.pallas_call(
        functools.partial(_mix_in_kernel, use_rope=use_rope),
        grid=(rows // tm,),
        in_specs=in_specs,
        out_specs=[row_spec(w) for w in widths],
        out_shape=[jax.ShapeDtypeStruct((rows, w), dt) for w, dt in zip(widths, dtypes)],
        compiler_params=_cparams(("arbitrary",)),
        name="mix_in_rope" if use_rope else "mix_in",
    )(*args)


ATTN_TILES_A = (256, 512, 2)
ATTN_TILES_C = (512, 512)
LOG2E = math.log2(math.e)


def _key_chunks(kv_refs, tk):
    chunks = []
    for s in range(len(kv_refs) // 2):
        n_rows = kv_refs[2 * s].shape[1]
        size = min(tk, n_rows)
        chunks += [(s, off, size) for off in range(0, n_rows, size)]
    return chunks


def _flash(q_get, n_chain, chunks, k_get, v_get, scratch):
    s_refs = [scratch[2 * c:2 * c + 2] for c in range(n_chain)]
    m_refs = scratch[2 * n_chain:3 * n_chain]
    acc_refs = scratch[3 * n_chain:4 * n_chain]

    def scores(i):
        size = chunks[i][2]
        for c in range(n_chain):
            s_refs[c][i % 2][:, 0:size] = _dot_nt(q_get(c), k_get(c, chunks[i]))

    def softmax_pv(i):
        size = chunks[i][2]
        for c in range(n_chain):
            slabs = [s_refs[c][i % 2][:, j * LANES:(j + 1) * LANES] for j in range(size // LANES)]
            m_new = functools.reduce(jnp.maximum, slabs)
            m_new = jnp.broadcast_to(jnp.max(m_new, axis=-1, keepdims=True), m_new.shape)
            if i > 0:
                m_old = m_refs[c][...]
                m_new = jnp.maximum(m_old, m_new)
            p = jnp.concatenate([jnp.exp2(sl - m_new).astype(BF16) for sl in slabs], axis=1)
            pv = _dot(p, v_get(c, chunks[i]))
            acc_refs[c][...] = pv if i == 0 else jnp.exp2(m_old - m_new) * acc_refs[c][...] + pv
            m_refs[c][...] = m_new

    scores(0)
    for i in range(len(chunks)):
        if i + 1 < len(chunks):
            scores(i + 1)
        softmax_pv(i)
    return acc_refs


def _attn_scratch(n_chain, m_rows, tk):
    return ([pltpu.VMEM((m_rows, tk), F32)] * (2 * n_chain) + [pltpu.VMEM((m_rows, LANES), F32)] * (2 * n_chain))


def _merge_halves(acc_low, acc_high):
    lane = lax.broadcasted_iota(jnp.int32, acc_low.shape, 1)
    low = lane < HEAD_DIM
    num = jnp.where(low, acc_low, acc_high)
    den = pltpu.roll(jnp.where(low, acc_high, acc_low), HEAD_DIM, 1)
    return num / den


def _gqa_kernel(q_ref, *refs, n_src, tk, n_sub):
    kv_refs, o_ref = refs[:2 * n_src], refs[2 * n_src]
    scratch = refs[2 * n_src + 1:]
    per_sub = len(scratch) // n_sub
    tq = q_ref.shape[1] // n_sub
    n_tiles = A_WIDTH // LANES
    low = lax.broadcasted_iota(jnp.int32, (tq, LANES), 1) < HEAD_DIM
    zero = jnp.zeros((tq, LANES), BF16)
    k_get = lambda g, ch: kv_refs[2 * ch[0]][0, ch[1]:ch[1] + ch[2], :]
    v_get = lambda g, ch: kv_refs[2 * ch[0] + 1][0, ch[1]:ch[1] + ch[2], g * LANES:(g + 1) * LANES]
    for t in range(n_sub):
        qs_ref, sub_scratch = scratch[t * per_sub], scratch[t * per_sub + 1:(t + 1) * per_sub]
        q_rows = slice(t * tq, (t + 1) * tq)
        for g in range(A_KV_HEADS):
            for c in range(n_tiles):
                qs_ref[g, c * tq:(c + 1) * tq, :] = jnp.where(
                    low == (g == 0), q_ref[0, q_rows, c * LANES:(c + 1) * LANES], zero)
        q_get = lambda g, qs_ref=qs_ref: qs_ref[g]
        acc = _flash(q_get, A_KV_HEADS, _key_chunks(kv_refs, tk), k_get, v_get, sub_scratch)
        for c in range(n_tiles):
            rows = slice(c * tq, (c + 1) * tq)
            o_ref[0, q_rows, c * LANES:(c + 1) * LANES] = _merge_halves(acc[0][rows, :], acc[1][rows, :]).astype(BF16)


def _gqa_attention(q, kv_list, tq, tk, n_sub):
    b, lq, _ = q.shape
    tk = min(tk, max(k.shape[1] for k, _ in kv_list))
    if lq < tq * n_sub:
        tq, n_sub = min(tq, lq), 1
    m_rows = (A_HEADS // A_KV_HEADS) * tq
    tq = tq * n_sub
    in_specs = [pl.BlockSpec((1, tq, A_WIDTH), lambda i, j: (i, j, 0))]
    args = [q]
    for k, v in kv_list:
        in_specs.append(pl.BlockSpec((1, k.shape[1], LANES), lambda i, j: (i, 0, 0)))
        in_specs.append(pl.BlockSpec((1, v.shape[1], 2 * LANES), lambda i, j: (i, 0, 0)))
        args += [k, v]
    return pl.pallas_call(
        functools.partial(_gqa_kernel, n_src=len(kv_list), tk=tk, n_sub=n_sub),
        grid=(b, lq // tq),
        in_specs=in_specs,
        out_specs=pl.BlockSpec((1, tq, A_WIDTH), lambda i, j: (i, j, 0)),
        out_shape=jax.ShapeDtypeStruct((b, lq, A_WIDTH), BF16),
        scratch_shapes=([pltpu.VMEM((A_KV_HEADS, m_rows, LANES), BF16)]
                        + _attn_scratch(A_KV_HEADS, m_rows, tk)) * n_sub,
        compiler_params=_cparams(("arbitrary", "arbitrary")),
        name="gqa_attention",
    )(*args)


def _mla_kernel(q_ref, *refs, n_src, tk):
    kv_refs, o_ref = refs[:2 * n_src], refs[2 * n_src]
    scratch = refs[2 * n_src + 1:]
    n_pairs = C_HEADS // 2
    per_pair = len(scratch) // n_pairs
    for hp in range(n_pairs):
        lanes = lambda e, hp=hp: slice((2 * hp + e) * LANES, (2 * hp + e + 1) * LANES)
        q_get = lambda e, lanes=lanes: q_ref[0, :, lanes(e)]
        k_get = lambda e, ch, lanes=lanes: kv_refs[2 * ch[0]][0, ch[1]:ch[1] + ch[2], lanes(e)]
        v_get = lambda e, ch, lanes=lanes: kv_refs[2 * ch[0] + 1][0, ch[1]:ch[1] + ch[2], lanes(e)]
        acc = _flash(q_get, 2, _key_chunks(kv_refs, tk), k_get, v_get, scratch[hp * per_pair:(hp + 1) * per_pair])
        o_ref[0, :, hp * LANES:(hp + 1) * LANES] = _merge_halves(acc[0][...], acc[1][...]).astype(BF16)


def _mla_attention(q, kv_list, tq, tk):
    b, lq, _ = q.shape
    tq, tk = min(tq, lq), min(tk, max(k.shape[1] for k, _ in kv_list))
    in_specs = [pl.BlockSpec((1, tq, CQ_PAD), lambda i, j: (i, j, 0))]
    args = [q]
    for k, v in kv_list:
        in_specs.append(pl.BlockSpec((1, k.shape[1], CQ_PAD), lambda i, j: (i, 0, 0)))
        in_specs.append(pl.BlockSpec((1, v.shape[1], CQ_PAD), lambda i, j: (i, 0, 0)))
        args += [k, v]
    return pl.pallas_call(
        functools.partial(_mla_kernel, n_src=len(kv_list), tk=tk),
        grid=(b, lq // tq),
        in_specs=in_specs,
        out_specs=pl.BlockSpec((1, tq, C_WIDTH), lambda i, j: (i, j, 0)),
        out_shape=jax.ShapeDtypeStruct((b, lq, C_WIDTH), BF16),
        scratch_shapes=_attn_scratch(2, tq, tk) * (C_HEADS // 2),
        compiler_params=_cparams(("arbitrary", "arbitrary")),
        name="mla_attention",
    )(*args)


def _fft_dims(seq):
    n = 2 * seq
    n2 = min(LANES, n // 16)
    return n // n2, n2


def _stack_real(m):
    return np.block([[m.real, -m.imag], [m.imag, m.real]]).astype(np.float32)


@functools.lru_cache(maxsize=None)
def _fft_constants(seq):
    n1, n2 = _fft_dims(seq)
    n = n1 * n2
    k1 = np.arange(n1)[:, None]
    f1 = np.exp(-2j * np.pi * k1 * np.arange(n1 // 2)[None, :] / n1)
    f2 = np.exp(-2j * np.pi * np.arange(n2)[:, None] * np.arange(n2)[None, :] / n2)
    tw = np.exp(-2j * np.pi * k1 * np.arange(n2)[None, :] / n)
    g1 = np.conj(f1).T / n
    return dict(
        m1=_stack_real(f1), m2=_stack_real(f2), m2_inv=_stack_real(np.conj(f2)), m1_inv=_stack_real(g1),
        tw_r=tw.real.astype(np.float32)[:, :, None], tw_i=tw.imag.astype(np.float32)[:, :, None])


def _hy_prep_kernel(v_ref, x1_ref, x0_ref, wv_ref, wx1_ref, wx0_ref, bv_ref, bx1_ref, bx0_ref, u_ref, x0c_ref):
    seq = v_ref.shape[1]
    row = lax.broadcasted_iota(jnp.int32, (seq, LANES), 0)
    first, last = row == 0, row == seq - 1

    def conv(p_ref, w_ref, b_ref):
        t = p_ref[0]
        prev = jnp.where(first, 0.0, pltpu.roll(t, 1, 0))
        nxt = jnp.where(last, 0.0, pltpu.roll(t, seq - 1, 0))
        return prev * w_ref[0:1] + t * w_ref[1:2] + nxt * w_ref[2:3] + b_ref[...]

    u_ref[0] = (conv(v_ref, wv_ref, bv_ref) * conv(x1_ref, wx1_ref, bx1_ref)).astype(BF16)
    x0c_ref[0] = conv(x0_ref, wx0_ref, bx0_ref).astype(BF16)


def _hy_prep(pb, conv_w, conv_b):
    b, seq, _ = pb.shape
    n_c = HY_WIDTH // LANES
    p_spec = lambda part: pl.BlockSpec((1, seq, LANES), lambda i, c: (i, 0, part * n_c + c))
    w_spec = lambda part: pl.BlockSpec((3, LANES), lambda i, c: (0, part * n_c + c))
    b_spec = lambda part: pl.BlockSpec((1, LANES), lambda i, c: (0, part * n_c + c))
    out_spec = pl.BlockSpec((1, seq, LANES), lambda i, c: (i, 0, c))
    cb = conv_b.reshape(1, B_COLS)
    return pl.pallas_call(
        _hy_prep_kernel,
        grid=(b, n_c),
        in_specs=[p_spec(0), p_spec(1), p_spec(2), w_spec(0), w_spec(1), w_spec(2), b_spec(0), b_spec(1), b_spec(2)],
        out_specs=[out_spec, out_spec],
        out_shape=[jax.ShapeDtypeStruct((b, seq, HY_WIDTH), BF16)] * 2,
        compiler_params=_cparams(("arbitrary", "arbitrary")),
        name="hyena_prep",
    )(pb, pb, pb, conv_w, conv_w, conv_w, cb, cb, cb)


def _hy_stage1_kernel(u_ref, m1_ref, a_ref):
    members, h, lc = u_ref.shape
    z = u_ref[...].astype(F32).reshape(members * h, lc).astype(BF16)
    a = _dot(m1_ref[...].astype(BF16), z)
    a_ref[0] = a.reshape(2, a.shape[0] // 2, lc).astype(BF16)


def _hy_stage1(u, seq, paired):
    b, _, ch = u.shape
    n1, n2 = _fft_dims(seq)
    cols = n2 * ch
    lc = min(cols, 4096)
    members = 2 if paired else 1
    m1 = _fft_constants(seq)["m1"]
    m1 = jnp.asarray(m1 if paired else m1[:, :n1 // 2])
    return pl.pallas_call(
        _hy_stage1_kernel,
        grid=(b // members, cols // lc),
        in_specs=[pl.BlockSpec((members, n1 // 2, lc), lambda j, c: (j, 0, c)),
                  pl.BlockSpec(m1.shape, lambda j, c: (0, 0))],
        out_specs=pl.BlockSpec((1, 2, n1, lc), lambda j, c: (j, 0, 0, c)),
        out_shape=jax.ShapeDtypeStruct((b // members, 2, n1, cols), BF16),
        compiler_params=_cparams(("arbitrary", "arbitrary")),
        name="hyena_stage1",
    )(u.reshape(b, n1 // 2, cols), m1)


def _twiddle(ar, ai, tr, ti):
    return ar * tr - ai * ti, ar * ti + ai * tr


HY_PLANES = 8


def _hy_filter_spec_kernel(a_ref, twr_ref, twi_ref, m2_ref, inv_ref, h_ref):
    n2 = a_ref.shape[3]
    m2 = m2_ref[...].astype(BF16)
    for p in range(a_ref.shape[2]):
        tr, ti = twr_ref[p], twi_ref[p]

        def fwd(seq_idx):
            br, bi = _twiddle(a_ref[seq_idx, 0, p].astype(F32), a_ref[seq_idx, 1, p].astype(F32), tr, ti)
            return _dot(m2, jnp.concatenate([br, bi], axis=0).astype(BF16))

        zf, zb = fwd(0), fwd(1)
        h_ref[p, 0:n2] = (zf[0:n2] + zb[0:n2]) * inv_ref[...]
        h_ref[p, n2:2 * n2] = (zf[n2:2 * n2] - zb[n2:2 * n2]) * inv_ref[...]


def _hy_filter_spec(a, inv_norm, seq):
    n1, n2 = _fft_dims(seq)
    cst = _fft_constants(seq)
    kp = min(HY_PLANES, n1)
    a5 = a.reshape(2, 2, n1, n2, HY_WIDTH)
    return pl.pallas_call(
        _hy_filter_spec_kernel,
        grid=(n1 // kp,),
        in_specs=[pl.BlockSpec((2, 2, kp, n2, HY_WIDTH), lambda k: (0, 0, k, 0, 0)),
                  pl.BlockSpec((kp, n2, 1), lambda k: (k, 0, 0)),
                  pl.BlockSpec((kp, n2, 1), lambda k: (k, 0, 0)),
                  pl.BlockSpec((2 * n2, 2 * n2), lambda k: (0, 0)),
                  pl.BlockSpec((1, HY_WIDTH), lambda k: (0, 0))],
        out_specs=pl.BlockSpec((kp, 2 * n2, HY_WIDTH), lambda k: (k, 0, 0)),
        out_shape=jax.ShapeDtypeStruct((n1, 2 * n2, HY_WIDTH), F32),
        compiler_params=_cparams(("arbitrary",)),
        name="hyena_filter_spectrum",
    )(a5, jnp.asarray(cst["tw_r"]), jnp.asarray(cst["tw_i"]), jnp.asarray(cst["m2"]), inv_norm)


def _hy_mid_kernel(a_ref, twr_ref, twi_ref, m2_ref, m2i_ref, h_ref, q_ref):
    n2 = a_ref.shape[3]
    m2, m2i = m2_ref[...].astype(BF16), m2i_ref[...].astype(BF16)
    for p in range(a_ref.shape[2]):
        tr, ti = twr_ref[p], twi_ref[p]
        br, bi = _twiddle(a_ref[0, 0, p].astype(F32), a_ref[0, 1, p].astype(F32), tr, ti)
        z = _dot(m2, jnp.concatenate([br, bi], axis=0).astype(BF16))
        zr, zi = z[0:n2], z[n2:2 * n2]
        hr, hi = h_ref[p, 0:n2], h_ref[p, n2:2 * n2]
        yr, yi = zr * hr - zi * hi, zr * hi + zi * hr
        y = _dot(m2i, jnp.concatenate([yr, yi], axis=0).astype(BF16))
        qr, qi = _twiddle(y[0:n2], y[n2:2 * n2], tr, -ti)
        q_ref[0, 0, p] = qr.astype(BF16)
        q_ref[0, 1, p] = qi.astype(BF16)


def _hy_mid(a, h_spec, seq):
    pairs = a.shape[0]
    n1, n2 = _fft_dims(seq)
    cst = _fft_constants(seq)
    kp = min(HY_PLANES, n1)
    a5 = a.reshape(pairs, 2, n1, n2, HY_WIDTH)
    blk = pl.BlockSpec((1, 2, kp, n2, HY_WIDTH), lambda k, j: (j, 0, k, 0, 0))
    q = pl.pallas_call(
        _hy_mid_kernel,
        grid=(n1 // kp, pairs),
        in_specs=[blk,
                  pl.BlockSpec((kp, n2, 1), lambda k, j: (k, 0, 0)),
                  pl.BlockSpec((kp, n2, 1), lambda k, j: (k, 0, 0)),
                  pl.BlockSpec((2 * n2, 2 * n2), lambda k, j: (0, 0)),
                  pl.BlockSpec((2 * n2, 2 * n2), lambda k, j: (0, 0)),
                  pl.BlockSpec((kp, 2 * n2, HY_WIDTH), lambda k, j: (k, 0, 0))],
        out_specs=blk,
        out_shape=jax.ShapeDtypeStruct((pairs, 2, n1, n2, HY_WIDTH), BF16),
        compiler_params=_cparams(("arbitrary", "arbitrary")),
        name="hyena_mid",
    )(a5, jnp.asarray(cst["tw_r"]), jnp.asarray(cst["tw_i"]), jnp.asarray(cst["m2"]), jnp.asarray(cst["m2_inv"]),
      h_spec)
    return q.reshape(pairs, 2, n1, n2 * HY_WIDTH)


def _hy_out_kernel(q_ref, m1i_ref, y_ref):
    _, _, n1, lc = q_ref.shape
    y = _dot(m1i_ref[...].astype(BF16), q_ref[0].reshape(2 * n1, lc))
    y_ref[...] = y.reshape(2, n1 // 2, lc).astype(BF16)


def _hy_out(q, seq):
    pairs, _, n1, cols = q.shape
    lc = min(cols, 4096)
    m1i = jnp.asarray(_fft_constants(seq)["m1_inv"])
    y = pl.pallas_call(
        _hy_out_kernel,
        grid=(pairs, cols // lc),
        in_specs=[pl.BlockSpec((1, 2, n1, lc), lambda j, c: (j, 0, 0, c)),
                  pl.BlockSpec((n1, 2 * n1), lambda j, c: (0, 0))],
        out_specs=pl.BlockSpec((2, n1 // 2, lc), lambda j, c: (j, 0, c)),
        out_shape=jax.ShapeDtypeStruct((2 * pairs, n1 // 2, cols), BF16),
        compiler_params=_cparams(("arbitrary", "arbitrary")),
        name="hyena_out",
    )(q, m1i)
    return y.reshape(2 * pairs, seq, HY_WIDTH)


def _hy_filter_kernel(z_ref, t_ref, w1_ref, b1_ref, w2_ref, b2_ref, w3_ref, b3_ref, w4_ref, fr_ref, dl_ref,
                      taps_ref, inv_ref):
    fr = fr_ref[...]
    hdn = jnp.sin(fr * (_dot_hp(z_ref[...], w1_ref[...]) + b1_ref[...]))
    hdn = jnp.sin(fr * (_dot_hp(hdn, w2_ref[...]) + b2_ref[...]))
    hdn = jnp.sin(fr * (_dot_hp(hdn, w3_ref[...]) + b3_ref[...]))
    h = _dot_hp(hdn, w4_ref[...]) * jnp.exp(-t_ref[...] * dl_ref[...])
    h_fwd, h_bwd = h[:, :HY_WIDTH], h[:, HY_WIDTH:]
    row = lax.broadcasted_iota(jnp.int32, h_bwd.shape, 0)
    h_bwd = jnp.where(row == 0, 0.0, h_bwd)
    norm = jnp.sum(jnp.abs(h_fwd), 0, keepdims=True) + jnp.sum(jnp.abs(h_bwd), 0, keepdims=True)
    inv_ref[...] = 1.0 / norm
    taps_ref[0] = h_fwd
    taps_ref[1] = h_bwd


def _hy_filter(seq, w1, b1, w2, b2, w3, b3, w4, freq):
    t = jnp.linspace(0.0, 1.0, seq, dtype=F32)[:, None]
    w = 2.0 * math.pi * jnp.arange(seq, dtype=F32)[:, None] / seq
    f = jnp.linspace(1e-4, HY_BANDS - 1, HY_BANDS, dtype=F32)[None, :]
    z = jnp.concatenate([t, jnp.cos(f * w), -jnp.sin(f * w)], -1)
    z = jnp.pad(z, ((0, 0), (0, HY_ORDER - HY_EMB)))
    w1p = jnp.pad(w1, ((0, HY_ORDER - HY_EMB), (0, 0)))
    deltas = jnp.abs(jnp.linspace(math.log(HY_TARGET) / HY_SLOW, math.log(HY_TARGET) / HY_FAST, HY_WIDTH, dtype=F32))
    row = lambda v: v.reshape(1, -1)
    return pl.pallas_call(
        _hy_filter_kernel,
        out_shape=[jax.ShapeDtypeStruct((2, seq, HY_WIDTH), F32), jax.ShapeDtypeStruct((1, HY_WIDTH), F32)],
        compiler_params=pltpu.CompilerParams(vmem_limit_bytes=VMEM_LIMIT),
        name="hyena_filter",
    )(z, t, w1p, row(b1), w2, row(b2), w3, row(b3), w4, row(freq), row(jnp.tile(deltas, 2)))


def _hyena(pb, hy, seq):
    conv_w, conv_b, w1, b1, w2, b2, w3, b3, w4, freq, d_skip = hy
    taps, inv_norm = _hy_filter(seq, w1, b1, w2, b2, w3, b3, w4, freq)
    h_spec = _hy_filter_spec(_hy_stage1(taps, seq, False), inv_norm, seq)
    u, x0c = _hy_prep(pb, conv_w, conv_b)
    y = _hy_out(_hy_mid(_hy_stage1(u, seq, True), h_spec, seq), seq)
    flat = lambda a: a.reshape(-1, HY_WIDTH)
    return flat(y), flat(u), flat(x0c), d_skip.reshape(1, HY_WIDTH)


def _mix_out_kernel(x_ref, mod_ref, oa_ref, y_ref, u_ref, x0_ref, d_ref, oc_ref, wa_ref, wb_ref, wc_ref, g_ref, b_ref,
                    o_ref):
    ob = (y_ref[...].astype(F32) + u_ref[...].astype(F32) * d_ref[...]) * x0_ref[...].astype(F32)
    mix = _dot(oa_ref[...], wa_ref[...]) + _dot(ob.astype(BF16), wb_ref[...]) + _dot(oc_ref[...], wc_ref[...])
    y = ALPHA * x_ref[...] + mod_ref[0, 0][2:3] * mix
    o_ref[...] = _layer_norm(y) * g_ref[...] + b_ref[...]


def _mix_out(x2, mods4, batch_of_tile, oa, hy_parts, oc, wa, wb, wc, g, b, tm):
    rows, d = x2.shape
    const = lambda i: (0, 0)
    row_spec = lambda w: pl.BlockSpec((tm, w), lambda i: (i, 0))
    return pl.pallas_call(
        _mix_out_kernel,
        grid=(rows // tm,),
        in_specs=[row_spec(d),
                  pl.BlockSpec((1, 1, 3, d), lambda i: (batch_of_tile(i), 1, 0, 0)),
                  row_spec(A_WIDTH), row_spec(HY_WIDTH), row_spec(HY_WIDTH), row_spec(HY_WIDTH),
                  pl.BlockSpec((1, HY_WIDTH), const), row_spec(C_WIDTH),
                  pl.BlockSpec((A_WIDTH, d), const), pl.BlockSpec((HY_WIDTH, d), const),
                  pl.BlockSpec((C_WIDTH, d), const),
                  pl.BlockSpec((1, d), const), pl.BlockSpec((1, d), const)],
        out_specs=row_spec(d),
        out_shape=jax.ShapeDtypeStruct((rows, d), F32),
        compiler_params=_cparams(("arbitrary",)),
        name="mix_out",
    )(x2, mods4, oa, *hy_parts, oc, wa, wb, wc, g.reshape(1, d), b.reshape(1, d))


def _take_cols(w, idx):
    idx = np.asarray(idx)
    return jnp.where(jnp.asarray(idx >= 0)[None, :], w[:, np.maximum(idx, 0)], 0.0)


def _mix_weights(w_in, w_out, a_qn, a_kn, q_g, kv_g, w_uq, w_ukv):
    col = np.arange
    q_cols = np.concatenate([col(h * HEAD_DIM, (h + 1) * HEAD_DIM) for h in A_HEAD_ORDER])
    pad = lambda n: -np.ones(n, np.int64)
    in_idx = np.concatenate([q_cols, col(A_WIDTH, OFF_KR), pad(HEAD_DIM), col(OFF_KR, OFF_KR + C_ROPE),
                             pad(LANES - HEAD_DIM - C_ROPE)])
    dq = C_NOPE + C_ROPE
    uq_idx = np.concatenate([np.concatenate([col(h * dq, (h + 1) * dq), pad(LANES - dq)]) for h in range(C_HEADS)])
    dkv = C_NOPE + C_V
    uk_idx = np.concatenate([np.concatenate([col(h * dkv, h * dkv + C_NOPE), pad(LANES - C_NOPE)])
                             for h in range(C_HEADS)])
    v_cols = lambda h: col(h * dkv + C_NOPE, (h + 1) * dkv)
    uv_idx = np.concatenate([np.concatenate([v_cols(h), pad(LANES - C_V)] if h % 2 == 0 else
                                            [pad(LANES - C_V), v_cols(h)]) for h in range(C_HEADS)])
    half = np.arange(LANES) // HEAD_DIM
    e = (half[:, None] == half[None, :]).astype(np.float32)
    tile2 = lambda g: jnp.tile(g, LANES // HEAD_DIM).reshape(1, LANES)
    return dict(
        w_in=_take_cols(w_in, in_idx).astype(BF16),
        e=jnp.asarray(e, BF16),
        gq=tile2(a_qn), gk=tile2(a_kn),
        cq_g=q_g.reshape(1, -1), ckv_g=kv_g.reshape(1, -1),
        w_uq=_take_cols(w_uq, uq_idx).astype(BF16),
        w_uk=_take_cols(w_ukv, uk_idx).astype(BF16),
        w_uv=_take_cols(w_ukv, uv_idx).astype(BF16),
        wa=w_out[q_cols].astype(BF16),
        wb=w_out[A_WIDTH:A_WIDTH + HY_WIDTH].astype(BF16),
        wc=w_out[A_WIDTH + HY_WIDTH:].astype(BF16),
    )


def _rope_tables(seq):
    rows = seq // GRID_W
    row = jnp.repeat(jnp.arange(rows, dtype=F32), GRID_W)
    colv = jnp.tile(jnp.arange(GRID_W, dtype=F32), rows)

    def cos_sin(rot_dim):
        n_freq = rot_dim // 4
        inv = ROPE_THETA ** (-jnp.arange(n_freq, dtype=F32) / n_freq)
        ang = jnp.concatenate([row[:, None] * inv, colv[:, None] * inv], -1)
        return jnp.cos(ang), jnp.sin(ang)

    cos, sin = cos_sin(HEAD_DIM)
    cos_a = jnp.tile(cos, (1, 2 * LANES // HEAD_DIM))
    sin_a = jnp.tile(jnp.concatenate([-sin, sin], -1), (1, LANES // HEAD_DIM))
    cos, sin = cos_sin(C_ROPE)
    ones = jnp.ones((seq, C_NOPE), F32)
    tail = LANES - C_NOPE - C_ROPE
    cos_c = jnp.concatenate([ones, cos, cos, jnp.ones((seq, tail), F32)], -1)
    sin_c = jnp.concatenate([0.0 * ones, -sin, sin, jnp.zeros((seq, tail), F32)], -1)
    return cos_a, sin_a, cos_c, sin_c


def kernel(x, c, ctx, c_ctx, ada_w, ada_b, ffn1_w_gu, ffn1_w_down, ffn2_w_gu, ffn2_w_down, ln_g, ln_b, w_in, w_out,
           a_q_norm, a_k_norm, hy_conv_w, hy_conv_b, hy_f_w1, hy_f_b1, hy_f_w2, hy_f_b2, hy_f_w3, hy_f_b3, hy_f_w4,
           hy_f_freq, hy_bias, mla_q_norm, mla_kv_norm, mla_w_uq, mla_w_ukv):
    b, seq, d = x.shape
    n_ctx = ctx.shape[1]
    depth = ada_w.shape[0]
    assert b % 2 == 0 and seq % GRID_W == 0

    tm = min(512, seq)
    tm_ctx = min(512, b * n_ctx)
    tiles_per_seq = seq // tm
    lat_batch = lambda i: i // tiles_per_seq
    ctx_batch = lambda i: b

    mod_rows = ((b + 1 + 7) // 8) * 8
    c_all = jnp.concatenate([c, c_ctx[None], jnp.zeros((mod_rows - b - 1, d), F32)], 0)
    mods = _ada_mod(c_all, ada_w, ada_b).reshape(depth, mod_rows, N_MOD // 3, 3, d)

    rope_tabs = _rope_tables(seq)
    x2 = x.reshape(b * seq, d)
    ctx2 = ctx.reshape(b * n_ctx, d)

    for l in range(depth):
        need_ctx = l < depth - 1
        m4 = mods[l]
        ffn1 = (ffn1_w_gu[l, :, :D_FF].astype(BF16), ffn1_w_gu[l, :, D_FF:].astype(BF16),
                ffn1_w_down[l].astype(BF16))
        ffn2 = (ffn2_w_gu[l, :, :D_FF].astype(BF16), ffn2_w_gu[l, :, D_FF:].astype(BF16),
                ffn2_w_down[l].astype(BF16))
        wts = _mix_weights(w_in[l], w_out[l], a_q_norm[l], a_k_norm[l], mla_q_norm[l], mla_kv_norm[l],
                           mla_w_uq[l], mla_w_ukv[l])
        hy = (hy_conv_w[l], hy_conv_b[l], hy_f_w1[l], hy_f_b1[l], hy_f_w2[l], hy_f_b2[l], hy_f_w3[l], hy_f_b3[l],
              hy_f_w4[l], hy_f_freq[l], hy_bias[l])

        x2 = _ffn_sublayer(x2, m4, 0, lat_batch, *ffn1, ln_g[l, 0], ln_b[l, 0], tm)
        ctx2 = _ffn_sublayer(ctx2, m4, 0, ctx_batch, *ffn1, ln_g[l, 0], ln_b[l, 0], tm_ctx)

        qa, ka, va, pb, qc, kc, vc = _mix_in(x2, m4, lat_batch, wts, rope_tabs, tiles_per_seq, tm)
        qa_c, ka_c, va_c, pb_c, qc_c, kc_c, vc_c = _mix_in(ctx2, m4, ctx_batch, wts, None, 1, tm_ctx)
        r3 = lambda a, n: a.reshape(b, n, a.shape[-1])
        ka_c, va_c, kc_c, vc_c = (r3(a, n_ctx) for a in (ka_c, va_c, kc_c, vc_c))

        oa = _gqa_attention(r3(qa, seq), [(ka_c, va_c), (r3(ka, seq), r3(va, seq))], *ATTN_TILES_A)
        oc = _mla_attention(r3(qc, seq), [(kc_c, vc_c), (r3(kc, seq), r3(vc, seq))], *ATTN_TILES_C)
        hy_parts = _hyena(r3(pb, seq), hy, seq)
        flat = lambda a: a.reshape(-1, a.shape[-1])
        x_mix = _mix_out(x2, m4, lat_batch, flat(oa), hy_parts, flat(oc), wts["wa"], wts["wb"], wts["wc"],
                         ln_g[l, 1], ln_b[l, 1], tm)
        if need_ctx:
            oa_c = _gqa_attention(r3(qa_c, n_ctx), [(ka_c, va_c)], *ATTN_TILES_A)
            oc_c = _mla_attention(r3(qc_c, n_ctx), [(kc_c, vc_c)], *ATTN_TILES_C)
            hy_parts_c = _hyena(r3(pb_c, n_ctx), hy, n_ctx)
            ctx2 = _mix_out(ctx2, m4, ctx_batch, flat(oa_c), hy_parts_c, flat(oc_c), wts["wa"], wts["wb"],
                            wts["wc"], ln_g[l, 1], ln_b[l, 1], tm_ctx)
            ctx2 = _ffn_sublayer(ctx2, m4, 2, ctx_batch, *ffn2, ln_g[l, 2], ln_b[l, 2], tm_ctx)
        x2 = _ffn_sublayer(x_mix, m4, 2, lat_batch, *ffn2, ln_g[l, 2], ln_b[l, 2], tm)
    return x2.reshape(b, seq, d)
```

```python
import functools
import math

import numpy as np
import jax
import jax.numpy as jnp
from jax import lax
from jax.experimental import pallas as pl
from jax.experimental.pallas import tpu as pltpu

F32 = jnp.float32
BF16 = jnp.bfloat16

D_MODEL = 1024
GRID_W = 64
N_MOD = 9
D_FF = 2816
DEPTH = 2
ALPHA = (2 * DEPTH) ** 0.25
EPS = 1e-6
A_HEADS = 6
A_KV_HEADS = 2
HEAD_DIM = 64
ROPE_THETA = 10000.0
HY_WIDTH = 256
HY_EMB = 33
HY_BANDS = (HY_EMB - 1) // 2
HY_ORDER = 64
HY_TARGET = 1e-2
HY_FAST = 0.3
HY_SLOW = 1.5
C_HEADS = 6
C_Q_LORA = 256
C_KV_LORA = 128
C_NOPE = 64
C_ROPE = 32
C_V = 64
A_WIDTH = A_HEADS * HEAD_DIM
C_WIDTH = C_HEADS * C_V
A_COLS = (A_HEADS + 2 * A_KV_HEADS) * HEAD_DIM
B_COLS = 3 * HY_WIDTH
C_COLS = C_Q_LORA + C_KV_LORA + C_ROPE

LANES = 128
MXU_DIM = 256
VMEM_LIMIT = 56 * 1024 * 1024

P_COLS = A_COLS + B_COLS + C_Q_LORA + C_KV_LORA + LANES
OFF_K = A_WIDTH
OFF_V = A_WIDTH + A_KV_HEADS * HEAD_DIM
OFF_B = A_COLS
OFF_CQ = A_COLS + B_COLS
OFF_CKV = OFF_CQ + C_Q_LORA
OFF_KR = OFF_CKV + C_KV_LORA
A_HEAD_ORDER = (0, 3, 1, 4, 2, 5)
CQ_PAD = C_HEADS * LANES


def _cparams(sem):
    return pltpu.CompilerParams(dimension_semantics=sem, vmem_limit_bytes=VMEM_LIMIT)


def _layer_norm(x):
    mu = jnp.mean(x, -1, keepdims=True)
    xc = x - mu
    var = jnp.mean(xc * xc, -1, keepdims=True)
    return xc * lax.rsqrt(var + EPS)


def _silu(x):
    return x * jax.nn.sigmoid(x)


def _dot(a, b):
    return jnp.dot(a, b, preferred_element_type=F32)


def _dot_nt(a, b):
    return lax.dot_general(a, b, (((1,), (1,)), ((), ())), preferred_element_type=F32)


def _dot_hp(a, b):
    return jnp.dot(a, b, preferred_element_type=F32, precision=lax.Precision.HIGHEST)


def _ada_kernel(c_ref, w_ref, b_ref, o_ref):
    a = _silu(c_ref[...]).astype(BF16)
    o_ref[0] = _dot(a, w_ref[0].astype(BF16)) + b_ref[0]


def _ada_mod(c_all, ada_w, ada_b):
    depth, d, n = ada_w.shape
    rows = c_all.shape[0]
    tn = n // 8
    return pl.pallas_call(
        _ada_kernel,
        grid=(depth, n // tn),
        in_specs=[
            pl.BlockSpec((rows, d), lambda l, j: (0, 0)),
            pl.BlockSpec((1, d, tn), lambda l, j: (l, 0, j)),
            pl.BlockSpec((1, 1, tn), lambda l, j: (l, 0, j)),
        ],
        out_specs=pl.BlockSpec((1, rows, tn), lambda l, j: (l, 0, j)),
        out_shape=jax.ShapeDtypeStruct((depth, rows, n), F32),
        compiler_params=_cparams(("arbitrary", "arbitrary")),
        name="ada_mod",
    )(c_all, ada_w, ada_b.reshape(depth, 1, n))


def _ffn_kernel(*refs, bounds, fused_mix):
    if fused_mix:
        (xin_ref, modm_ref, oa_ref, y_ref, u_ref, x0_ref, d_ref, oc_ref, wa_ref, wb_ref, wc_ref, g1_ref, b1_ref,
         mod_ref, wg_ref, wu_ref, wd_ref, g_ref, b_ref, o_ref, x_ref) = refs
    else:
        x_ref, mod_ref, wg_ref, wu_ref, wd_ref, g_ref, b_ref, o_ref = refs
    mod = mod_ref[0, 0]
    half = x_ref.shape[0] // 2
    halves = (slice(0, half), slice(half, 2 * half))
    n_chunks = len(bounds) - 1

    def mix_rows(r):
        ob = (y_ref[r, :].astype(F32) + u_ref[r, :].astype(F32) * d_ref[...]) * x0_ref[r, :].astype(F32)
        mix = (_dot(oa_ref[r, :], wa_ref[...]) + _dot(ob.astype(BF16), wb_ref[...])
               + _dot(oc_ref[r, :], wc_ref[...]))
        y = ALPHA * xin_ref[r, :] + modm_ref[0, 0][2:3] * mix
        x_ref[r, :] = _layer_norm(y) * g1_ref[...] + b1_ref[...]

    hs = []
    for r in halves:
        if fused_mix:
            mix_rows(r)
        hs.append((_layer_norm(x_ref[r, :]) * (1.0 + mod[1:2]) + mod[0:1]).astype(BF16))
    acc = None
    for k, (lo, hi) in enumerate(zip(bounds[:-1], bounds[1:])):
        if k == 0:
            a = jnp.concatenate([(_silu(_dot(hr, wg_ref[:, lo:hi])) * _dot(hr, wu_ref[:, lo:hi])).astype(BF16)
                                 for hr in hs], axis=0)
            h = jnp.concatenate(hs, axis=0)
        else:
            a = (_silu(_dot(h, wg_ref[:, lo:hi])) * _dot(h, wu_ref[:, lo:hi])).astype(BF16)
        if k == n_chunks - 1:
            for r in halves:
                part = _dot(a[r, :], wd_ref[lo:hi, :])
                y = ALPHA * x_ref[r, :] + (0.5 * mod[2:3]) * (part if acc is None else acc[r, :] + part)
                o_ref[r, :] = _layer_norm(y) * g_ref[...] + b_ref[...]
        else:
            part = _dot(a, wd_ref[lo:hi, :])
            acc = part if acc is None else acc + part


def _ffn_sublayer(x2, mods4, mod_group, batch_of_tile, wg, wu, wd, g, b, tm, mix=None):
    rows, d = x2.shape
    f = wg.shape[1]
    const = lambda i: (0, 0)
    row_spec = lambda w: pl.BlockSpec((tm, w), lambda i: (i, 0))
    mod_spec = lambda group: pl.BlockSpec((1, 1, 3, d), lambda i: (batch_of_tile(i), group, 0, 0))
    vec_spec = pl.BlockSpec((1, d), const)
    n_tiles = -(-f // MXU_DIM)
    bounds = (0, min(f, ((n_tiles + 1) // 2) * MXU_DIM), f)
    in_specs, args, scratch = [row_spec(d)], [x2], []
    if mix is not None:
        oa, (y, u, x0c, d_skip), oc, wa, wb, wc, g1, b1 = mix
        in_specs += [mod_spec(1), row_spec(A_WIDTH), row_spec(HY_WIDTH), row_spec(HY_WIDTH), row_spec(HY_WIDTH),
                     pl.BlockSpec((1, HY_WIDTH), const), row_spec(C_WIDTH),
                     pl.BlockSpec((A_WIDTH, d), const), pl.BlockSpec((HY_WIDTH, d), const),
                     pl.BlockSpec((C_WIDTH, d), const), vec_spec, vec_spec]
        args += [mods4, oa, y, u, x0c, d_skip, oc, wa, wb, wc, g1.reshape(1, d), b1.reshape(1, d)]
        scratch = [pltpu.VMEM((tm, d), F32)]
    in_specs += [mod_spec(mod_group),
                 pl.BlockSpec((d, f), const, pipeline_mode=pl.Buffered(1)),
                 pl.BlockSpec((d, f), const, pipeline_mode=pl.Buffered(1)),
                 pl.BlockSpec((f, d), const, pipeline_mode=pl.Buffered(1)),
                 vec_spec, vec_spec]
    args += [mods4, wg, wu, wd, g.reshape(1, d), b.reshape(1, d)]
    return pl.pallas_call(
        functools.partial(_ffn_kernel, bounds=bounds, fused_mix=mix is not None),
        grid=(rows // tm,),
        in_specs=in_specs,
        out_specs=row_spec(d),
        out_shape=jax.ShapeDtypeStruct((rows, d), F32),
        scratch_shapes=scratch,
        compiler_params=_cparams(("arbitrary",)),
        name="mix_out_ffn" if mix is not None else "ffn_sublayer",
    )(*args)


def _rope(x, cos, sin_signed, half):
    lane = lax.broadcasted_iota(jnp.int32, x.shape, 1)
    first = (lane % (2 * half)) < half
    partner = jnp.where(first, pltpu.roll(x, LANES - half, 1), pltpu.roll(x, half, 1))
    return x * cos + partner * sin_signed


def _half_tile_mean_sq(x, e_ref):
    return _dot((x * x).astype(BF16), e_ref[...]) * (1.0 / HEAD_DIM)


def _mix_in_kernel(x_ref, mod_ref, w_ref, e_ref, gq_ref, gk_ref, cq_g_ref, ckv_g_ref, wuq_ref, wuk_ref, wuv_ref,
                   *rest, use_rope):
    if use_rope:
        cos_a_ref, sin_a_ref, cos_c_ref, sin_c_ref = rest[:4]
        rest = rest[4:]
    qa_ref, ka_ref, va_ref, pb_ref, qc_ref, kc_ref, vc_ref = rest

    mod = mod_ref[0, 0]
    half = x_ref.shape[0] // 2
    scale_c = LOG2E * (C_NOPE + C_ROPE) ** -0.5
    for r in (slice(0, half), slice(half, 2 * half)):
        h = (_layer_norm(x_ref[r, :]) * (1.0 + mod[1:2]) + mod[0:1]).astype(BF16)
        p = _dot(h, w_ref[...])

        def rope_a(t):
            return _rope(t, cos_a_ref[r, :], sin_a_ref[r, :], HEAD_DIM // 2) if use_rope else t

        def rope_c(t):
            return _rope(t, cos_c_ref[r, :], sin_c_ref[r, :], C_ROPE // 2) if use_rope else t

        for c in range(A_WIDTH // LANES):
            t = p[:, c * LANES:(c + 1) * LANES]
            t = t * lax.rsqrt(_half_tile_mean_sq(t, e_ref) + EPS) * gq_ref[...]
            qa_ref[r, c * LANES:(c + 1) * LANES] = (rope_a(t) * (LOG2E * HEAD_DIM ** -0.5)).astype(BF16)
        t = p[:, OFF_K:OFF_K + LANES]
        t = t * lax.rsqrt(_half_tile_mean_sq(t, e_ref) + EPS) * gk_ref[...]
        ka_ref[r, :] = rope_a(t).astype(BF16)
        v = p[:, OFF_V:OFF_V + LANES]
        low = lax.broadcasted_iota(jnp.int32, v.shape, 1) < HEAD_DIM
        va_ref[r, 0:LANES] = jnp.where(low, v, 1.0).astype(BF16)
        va_ref[r, LANES:2 * LANES] = jnp.where(low, 1.0, v).astype(BF16)

        pb_ref[r, :] = p[:, OFF_B:OFF_B + B_COLS]

        c_q = p[:, OFF_CQ:OFF_CQ + C_Q_LORA]
        c_q = c_q * lax.rsqrt(jnp.mean(c_q * c_q, -1, keepdims=True) + EPS) * cq_g_ref[...]
        qq = _dot(c_q.astype(BF16), wuq_ref[...])
        c_kv = p[:, OFF_CKV:OFF_CKV + C_KV_LORA]
        c_kv = (c_kv * lax.rsqrt(jnp.mean(c_kv * c_kv, -1, keepdims=True) + EPS) * ckv_g_ref[...]).astype(BF16)
        kn = _dot(c_kv, wuk_ref[...])
        vv = _dot(c_kv, wuv_ref[...])
        k_r = rope_c(p[:, OFF_KR:OFF_KR + LANES])
        for hd in range(C_HEADS):
            sl = slice(hd * LANES, (hd + 1) * LANES)
            qc_ref[r, sl] = (rope_c(qq[:, sl]) * scale_c).astype(BF16)
            kc_ref[r, sl] = (kn[:, sl] + k_r).astype(BF16)
            ones_low = (hd % 2) == 1
            vc_ref[r, sl] = jnp.where(low != ones_low, vv[:, sl], 1.0).astype(BF16)


def _mix_in(x2, mods4, batch_of_tile, wts, rope_tabs, tiles_per_seq, tm):
    rows, d = x2.shape
    const = lambda i: (0, 0)
    use_rope = rope_tabs is not None
    in_specs = [
        pl.BlockSpec((tm, d), lambda i: (i, 0)),
        pl.BlockSpec((1, 1, 3, d), lambda i: (batch_of_tile(i), 1, 0, 0)),
        pl.BlockSpec((d, P_COLS), const, pipeline_mode=pl.Buffered(1)),
        pl.BlockSpec((LANES, LANES), const),
        pl.BlockSpec((1, LANES), const),
        pl.BlockSpec((1, LANES), const),
        pl.BlockSpec((1, C_Q_LORA), const),
        pl.BlockSpec((1, C_KV_LORA), const),
        pl.BlockSpec((C_Q_LORA, CQ_PAD), const),
        pl.BlockSpec((C_KV_LORA, CQ_PAD), const),
        pl.BlockSpec((C_KV_LORA, CQ_PAD), const),
    ]
    args = [x2, mods4, wts["w_in"], wts["e"], wts["gq"], wts["gk"], wts["cq_g"], wts["ckv_g"],
            wts["w_uq"], wts["w_uk"], wts["w_uv"]]
    if use_rope:
        in_specs += [pl.BlockSpec((tm, LANES), lambda i: (i % tiles_per_seq, 0))] * 4
        args += list(rope_tabs)
    row_spec = lambda w: pl.BlockSpec((tm, w), lambda i: (i, 0))
    widths = (A_WIDTH, LANES, 2 * LANES, B_COLS, CQ_PAD, CQ_PAD, CQ_PAD)
    dtypes = (BF16, BF16, BF16, F32, BF16, BF16, BF16)
    return pl.pallas_call(
        functools.partial(_mix_in_kernel, use_rope=use_rope),
        grid=(rows // tm,),
        in_specs=in_specs,
        out_specs=[row_spec(w) for w in widths],
        out_shape=[jax.ShapeDtypeStruct((rows, w), dt) for w, dt in zip(widths, dtypes)],
        compiler_params=_cparams(("arbitrary",)),
        name="mix_in_rope" if use_rope else "mix_in",
    )(*args)


ATTN_TILES_A = (256, 512, 2)
ATTN_TILES_C = (512, 512)
LOG2E = math.log2(math.e)


def _key_chunks(kv_refs, tk):
    chunks = []
    for s in range(len(kv_refs) // 2):
        n_rows = kv_refs[2 * s].shape[1]
        size = min(tk, n_rows)
        chunks += [(s, off, size) for off in range(0, n_rows, size)]
    return chunks


def _flash(q_get, n_chain, chunks, k_get, v_get, scratch):
    s_refs = [scratch[2 * c:2 * c + 2] for c in range(n_chain)]
    m_refs = scratch[2 * n_chain:3 * n_chain]
    acc_refs = scratch[3 * n_chain:4 * n_chain]

    def scores(i):
        size = chunks[i][2]
        for c in range(n_chain):
            s_refs[c][i % 2][:, 0:size] = _dot_nt(q_get(c), k_get(c, chunks[i]))

    def softmax_pv(i):
        size = chunks[i][2]
        for c in range(n_chain):
            slabs = [s_refs[c][i % 2][:, j * LANES:(j + 1) * LANES] for j in range(size // LANES)]
            m_new = functools.reduce(jnp.maximum, slabs)
            m_new = jnp.broadcast_to(jnp.max(m_new, axis=-1, keepdims=True), m_new.shape)
            if i > 0:
                m_old = m_refs[c][...]
                m_new = jnp.maximum(m_old, m_new)
            p = jnp.concatenate([jnp.exp2(sl - m_new).astype(BF16) for sl in slabs], axis=1)
            pv = _dot(p, v_get(c, chunks[i]))
            acc_refs[c][...] = pv if i == 0 else jnp.exp2(m_old - m_new) * acc_refs[c][...] + pv
            m_refs[c][...] = m_new

    scores(0)
    for i in range(len(chunks)):
        if i + 1 < len(chunks):
            scores(i + 1)
        softmax_pv(i)
    return acc_refs


def _attn_scratch(n_chain, m_rows, tk):
    return ([pltpu.VMEM((m_rows, tk), F32)] * (2 * n_chain) + [pltpu.VMEM((m_rows, LANES), F32)] * (2 * n_chain))


def _merge_halves(acc_low, acc_high):
    lane = lax.broadcasted_iota(jnp.int32, acc_low.shape, 1)
    low = lane < HEAD_DIM
    num = jnp.where(low, acc_low, acc_high)
    den = pltpu.roll(jnp.where(low, acc_high, acc_low), HEAD_DIM, 1)
    return num / den


def _gqa_kernel(q_ref, *refs, n_src, tk, n_sub):
    kv_refs, o_ref = refs[:2 * n_src], refs[2 * n_src]
    scratch = refs[2 * n_src + 1:]
    per_sub = len(scratch) // n_sub
    tq = q_ref.shape[1] // n_sub
    n_tiles = A_WIDTH // LANES
    low = lax.broadcasted_iota(jnp.int32, (tq, LANES), 1) < HEAD_DIM
    zero = jnp.zeros((tq, LANES), BF16)
    k_get = lambda g, ch: kv_refs[2 * ch[0]][0, ch[1]:ch[1] + ch[2], :]
    v_get = lambda g, ch: kv_refs[2 * ch[0] + 1][0, ch[1]:ch[1] + ch[2], g * LANES:(g + 1) * LANES]
    for t in range(n_sub):
        qs_ref, sub_scratch = scratch[t * per_sub], scratch[t * per_sub + 1:(t + 1) * per_sub]
        q_rows = slice(t * tq, (t + 1) * tq)
        for g in range(A_KV_HEADS):
            for c in range(n_tiles):
                qs_ref[g, c * tq:(c + 1) * tq, :] = jnp.where(
                    low == (g == 0), q_ref[0, q_rows, c * LANES:(c + 1) * LANES], zero)
        q_get = lambda g, qs_ref=qs_ref: qs_ref[g]
        acc = _flash(q_get, A_KV_HEADS, _key_chunks(kv_refs, tk), k_get, v_get, sub_scratch)
        for c in range(n_tiles):
            rows = slice(c * tq, (c + 1) * tq)
            o_ref[0, q_rows, c * LANES:(c + 1) * LANES] = _merge_halves(acc[0][rows, :], acc[1][rows, :]).astype(BF16)


def _gqa_attention(q, kv_list, tq, tk, n_sub):
    b, lq, _ = q.shape
    tk = min(tk, max(k.shape[1] for k, _ in kv_list))
    if lq < tq * n_sub:
        tq, n_sub = min(tq, lq), 1
    m_rows = (A_HEADS // A_KV_HEADS) * tq
    tq = tq * n_sub
    in_specs = [pl.BlockSpec((1, tq, A_WIDTH), lambda i, j: (i, j, 0))]
    args = [q]
    for k, v in kv_list:
        in_specs.append(pl.BlockSpec((1, k.shape[1], LANES), lambda i, j: (i, 0, 0)))
        in_specs.append(pl.BlockSpec((1, v.shape[1], 2 * LANES), lambda i, j: (i, 0, 0)))
        args += [k, v]
    return pl.pallas_call(
        functools.partial(_gqa_kernel, n_src=len(kv_list), tk=tk, n_sub=n_sub),
        grid=(b, lq // tq),
        in_specs=in_specs,
        out_specs=pl.BlockSpec((1, tq, A_WIDTH), lambda i, j: (i, j, 0)),
        out_shape=jax.ShapeDtypeStruct((b, lq, A_WIDTH), BF16),
        scratch_shapes=([pltpu.VMEM((A_KV_HEADS, m_rows, LANES), BF16)]
                        + _attn_scratch(A_KV_HEADS, m_rows, tk)) * n_sub,
        compiler_params=_cparams(("arbitrary", "arbitrary")),
        name="gqa_attention",
    )(*args)


def _mla_kernel(q_ref, *refs, n_src, tk):
    kv_refs, o_ref = refs[:2 * n_src], refs[2 * n_src]
    scratch = refs[2 * n_src + 1:]
    n_pairs = C_HEADS // 2
    per_pair = len(scratch) // n_pairs
    for hp in range(n_pairs):
        lanes = lambda e, hp=hp: slice((2 * hp + e) * LANES, (2 * hp + e + 1) * LANES)
        q_get = lambda e, lanes=lanes: q_ref[0, :, lanes(e)]
        k_get = lambda e, ch, lanes=lanes: kv_refs[2 * ch[0]][0, ch[1]:ch[1] + ch[2], lanes(e)]
        v_get = lambda e, ch, lanes=lanes: kv_refs[2 * ch[0] + 1][0, ch[1]:ch[1] + ch[2], lanes(e)]
        acc = _flash(q_get, 2, _key_chunks(kv_refs, tk), k_get, v_get, scratch[hp * per_pair:(hp + 1) * per_pair])
        o_ref[0, :, hp * LANES:(hp + 1) * LANES] = _merge_halves(acc[0][...], acc[1][...]).astype(BF16)


def _mla_attention(q, kv_list, tq, tk):
    b, lq, _ = q.shape
    tq, tk = min(tq, lq), min(tk, max(k.shape[1] for k, _ in kv_list))
    in_specs = [pl.BlockSpec((1, tq, CQ_PAD), lambda i, j: (i, j, 0))]
    args = [q]
    for k, v in kv_list:
        in_specs.append(pl.BlockSpec((1, k.shape[1], CQ_PAD), lambda i, j: (i, 0, 0)))
        in_specs.append(pl.BlockSpec((1, v.shape[1], CQ_PAD), lambda i, j: (i, 0, 0)))
        args += [k, v]
    return pl.pallas_call(
        functools.partial(_mla_kernel, n_src=len(kv_list), tk=tk),
        grid=(b, lq // tq),
        in_specs=in_specs,
        out_specs=pl.BlockSpec((1, tq, C_WIDTH), lambda i, j: (i, j, 0)),
        out_shape=jax.ShapeDtypeStruct((b, lq, C_WIDTH), BF16),
        scratch_shapes=_attn_scratch(2, tq, tk) * (C_HEADS // 2),
        compiler_params=_cparams(("arbitrary", "arbitrary")),
        name="mla_attention",
    )(*args)


def _fft_dims(seq):
    n = 2 * seq
    n2 = min(LANES, n // 16)
    return n // n2, n2


def _stack_real(m):
    return np.block([[m.real, -m.imag], [m.imag, m.real]]).astype(np.float32)


@functools.lru_cache(maxsize=None)
def _fft_constants(seq):
    n1, n2 = _fft_dims(seq)
    n = n1 * n2
    k1 = np.arange(n1)[:, None]
    f1 = np.exp(-2j * np.pi * k1 * np.arange(n1 // 2)[None, :] / n1)
    f2 = np.exp(-2j * np.pi * np.arange(n2)[:, None] * np.arange(n2)[None, :] / n2)
    tw = np.exp(-2j * np.pi * k1 * np.arange(n2)[None, :] / n)
    g1 = np.conj(f1).T / n
    return dict(
        m1=_stack_real(f1), m2=_stack_real(f2), m2_inv=_stack_real(np.conj(f2)), m1_inv=_stack_real(g1),
        tw_r=tw.real.astype(np.float32)[:, :, None], tw_i=tw.imag.astype(np.float32)[:, :, None])


def _hy_prep_kernel(v_ref, x1_ref, x0_ref, wv_ref, wx1_ref, wx0_ref, bv_ref, bx1_ref, bx0_ref, u_ref, x0c_ref):
    seq = v_ref.shape[1]
    row = lax.broadcasted_iota(jnp.int32, (seq, LANES), 0)
    first, last = row == 0, row == seq - 1

    def conv(p_ref, w_ref, b_ref):
        t = p_ref[0]
        prev = jnp.where(first, 0.0, pltpu.roll(t, 1, 0))
        nxt = jnp.where(last, 0.0, pltpu.roll(t, seq - 1, 0))
        return prev * w_ref[0:1] + t * w_ref[1:2] + nxt * w_ref[2:3] + b_ref[...]

    u_ref[0] = (conv(v_ref, wv_ref, bv_ref) * conv(x1_ref, wx1_ref, bx1_ref)).astype(BF16)
    x0c_ref[0] = conv(x0_ref, wx0_ref, bx0_ref).astype(BF16)


def _hy_prep(pb, conv_w, conv_b):
    b, seq, _ = pb.shape
    n_c = HY_WIDTH // LANES
    p_spec = lambda part: pl.BlockSpec((1, seq, LANES), lambda i, c: (i, 0, part * n_c + c))
    w_spec = lambda part: pl.BlockSpec((3, LANES), lambda i, c: (0, part * n_c + c))
    b_spec = lambda part: pl.BlockSpec((1, LANES), lambda i, c: (0, part * n_c + c))
    out_spec = pl.BlockSpec((1, seq, LANES), lambda i, c: (i, 0, c))
    cb = conv_b.reshape(1, B_COLS)
    return pl.pallas_call(
        _hy_prep_kernel,
        grid=(b, n_c),
        in_specs=[p_spec(0), p_spec(1), p_spec(2), w_spec(0), w_spec(1), w_spec(2), b_spec(0), b_spec(1), b_spec(2)],
        out_specs=[out_spec, out_spec],
        out_shape=[jax.ShapeDtypeStruct((b, seq, HY_WIDTH), BF16)] * 2,
        compiler_params=_cparams(("arbitrary", "arbitrary")),
        name="hyena_prep",
    )(pb, pb, pb, conv_w, conv_w, conv_w, cb, cb, cb)


def _hy_stage1_kernel(u_ref, m1_ref, a_ref):
    members, h, lc = u_ref.shape
    z = u_ref[...].astype(F32).reshape(members * h, lc).astype(BF16)
    a = _dot(m1_ref[...].astype(BF16), z)
    a_ref[0] = a.reshape(2, a.shape[0] // 2, lc).astype(BF16)


def _hy_stage1(u, seq, paired):
    b, _, ch = u.shape
    n1, n2 = _fft_dims(seq)
    cols = n2 * ch
    lc = min(cols, 4096)
    members = 2 if paired else 1
    m1 = _fft_constants(seq)["m1"]
    m1 = jnp.asarray(m1 if paired else m1[:, :n1 // 2])
    return pl.pallas_call(
        _hy_stage1_kernel,
        grid=(b // members, cols // lc),
        in_specs=[pl.BlockSpec((members, n1 // 2, lc), lambda j, c: (j, 0, c)),
                  pl.BlockSpec(m1.shape, lambda j, c: (0, 0))],
        out_specs=pl.BlockSpec((1, 2, n1, lc), lambda j, c: (j, 0, 0, c)),
        out_shape=jax.ShapeDtypeStruct((b // members, 2, n1, cols), BF16),
        compiler_params=_cparams(("arbitrary", "arbitrary")),
        name="hyena_stage1",
    )(u.reshape(b, n1 // 2, cols), m1)


def _twiddle(ar, ai, tr, ti):
    return ar * tr - ai * ti, ar * ti + ai * tr


HY_PLANES = 8


def _hy_filter_spec_kernel(a_ref, twr_ref, twi_ref, m2_ref, inv_ref, h_ref):
    n2 = a_ref.shape[3]
    m2 = m2_ref[...].astype(BF16)
    for p in range(a_ref.shape[2]):
        tr, ti = twr_ref[p], twi_ref[p]

        def fwd(seq_idx):
            br, bi = _twiddle(a_ref[seq_idx, 0, p].astype(F32), a_ref[seq_idx, 1, p].astype(F32), tr, ti)
            return _dot(m2, jnp.concatenate([br, bi], axis=0).astype(BF16))

        zf, zb = fwd(0), fwd(1)
        h_ref[p, 0:n2] = (zf[0:n2] + zb[0:n2]) * inv_ref[...]
        h_ref[p, n2:2 * n2] = (zf[n2:2 * n2] - zb[n2:2 * n2]) * inv_ref[...]


def _hy_filter_spec(a, inv_norm, seq):
    n1, n2 = _fft_dims(seq)
    cst = _fft_constants(seq)
    kp = min(HY_PLANES, n1)
    a5 = a.reshape(2, 2, n1, n2, HY_WIDTH)
    return pl.pallas_call(
        _hy_filter_spec_kernel,
        grid=(n1 // kp,),
        in_specs=[pl.BlockSpec((2, 2, kp, n2, HY_WIDTH), lambda k: (0, 0, k, 0, 0)),
                  pl.BlockSpec((kp, n2, 1), lambda k: (k, 0, 0)),
                  pl.BlockSpec((kp, n2, 1), lambda k: (k, 0, 0)),
                  pl.BlockSpec((2 * n2, 2 * n2), lambda k: (0, 0)),
                  pl.BlockSpec((1, HY_WIDTH), lambda k: (0, 0))],
        out_specs=pl.BlockSpec((kp, 2 * n2, HY_WIDTH), lambda k: (k, 0, 0)),
        out_shape=jax.ShapeDtypeStruct((n1, 2 * n2, HY_WIDTH), F32),
        compiler_params=_cparams(("arbitrary",)),
        name="hyena_filter_spectrum",
    )(a5, jnp.asarray(cst["tw_r"]), jnp.asarray(cst["tw_i"]), jnp.asarray(cst["m2"]), inv_norm)


def _hy_mid_kernel(a_ref, twr_ref, twi_ref, m2_ref, m2i_ref, h_ref, q_ref):
    n2 = a_ref.shape[3]
    m2, m2i = m2_ref[...].astype(BF16), m2i_ref[...].astype(BF16)
    for p in range(a_ref.shape[2]):
        tr, ti = twr_ref[p], twi_ref[p]
        br, bi = _twiddle(a_ref[0, 0, p].astype(F32), a_ref[0, 1, p].astype(F32), tr, ti)
        z = _dot(m2, jnp.concatenate([br, bi], axis=0).astype(BF16))
        zr, zi = z[0:n2], z[n2:2 * n2]
        hr, hi = h_ref[p, 0:n2], h_ref[p, n2:2 * n2]
        yr, yi = zr * hr - zi * hi, zr * hi + zi * hr
        y = _dot(m2i, jnp.concatenate([yr, yi], axis=0).astype(BF16))
        qr, qi = _twiddle(y[0:n2], y[n2:2 * n2], tr, -ti)
        q_ref[0, 0, p] = qr.astype(BF16)
        q_ref[0, 1, p] = qi.astype(BF16)


def _hy_mid(a, h_spec, seq):
    pairs = a.shape[0]
    n1, n2 = _fft_dims(seq)
    cst = _fft_constants(seq)
    kp = min(HY_PLANES, n1)
    a5 = a.reshape(pairs, 2, n1, n2, HY_WIDTH)
    blk = pl.BlockSpec((1, 2, kp, n2, HY_WIDTH), lambda k, j: (j, 0, k, 0, 0))
    q = pl.pallas_call(
        _hy_mid_kernel,
        grid=(n1 // kp, pairs),
        in_specs=[blk,
                  pl.BlockSpec((kp, n2, 1), lambda k, j: (k, 0, 0)),
                  pl.BlockSpec((kp, n2, 1), lambda k, j: (k, 0, 0)),
                  pl.BlockSpec((2 * n2, 2 * n2), lambda k, j: (0, 0)),
                  pl.BlockSpec((2 * n2, 2 * n2), lambda k, j: (0, 0)),
                  pl.BlockSpec((kp, 2 * n2, HY_WIDTH), lambda k, j: (k, 0, 0))],
        out_specs=blk,
        out_shape=jax.ShapeDtypeStruct((pairs, 2, n1, n2, HY_WIDTH), BF16),
        compiler_params=_cparams(("arbitrary", "arbitrary")),
        name="hyena_mid",
    )(a5, jnp.asarray(cst["tw_r"]), jnp.asarray(cst["tw_i"]), jnp.asarray(cst["m2"]), jnp.asarray(cst["m2_inv"]),
      h_spec)
    return q.reshape(pairs, 2, n1, n2 * HY_WIDTH)


def _hy_out_kernel(q_ref, m1i_ref, y_ref):
    _, _, n1, lc = q_ref.shape
    y = _dot(m1i_ref[...].astype(BF16), q_ref[0].reshape(2 * n1, lc))
    y_ref[...] = y.reshape(2, n1 // 2, lc).astype(BF16)


def _hy_out(q, seq):
    pairs, _, n1, cols = q.shape
    lc = min(cols, 4096)
    m1i = jnp.asarray(_fft_constants(seq)["m1_inv"])
    y = pl.pallas_call(
        _hy_out_kernel,
        grid=(pairs, cols // lc),
        in_specs=[pl.BlockSpec((1, 2, n1, lc), lambda j, c: (j, 0, 0, c)),
                  pl.BlockSpec((n1, 2 * n1), lambda j, c: (0, 0))],
        out_specs=pl.BlockSpec((2, n1 // 2, lc), lambda j, c: (j, 0, c)),
        out_shape=jax.ShapeDtypeStruct((2 * pairs, n1 // 2, cols), BF16),
        compiler_params=_cparams(("arbitrary", "arbitrary")),
        name="hyena_out",
    )(q, m1i)
    return y.reshape(2 * pairs, seq, HY_WIDTH)


def _hy_filter_kernel(z_ref, t_ref, w1_ref, b1_ref, w2_ref, b2_ref, w3_ref, b3_ref, w4_ref, fr_ref, dl_ref,
                      taps_ref, inv_ref):
    fr = fr_ref[...]
    hdn = jnp.sin(fr * (_dot_hp(z_ref[...], w1_ref[...]) + b1_ref[...]))
    hdn = jnp.sin(fr * (_dot_hp(hdn, w2_ref[...]) + b2_ref[...]))
    hdn = jnp.sin(fr * (_dot_hp(hdn, w3_ref[...]) + b3_ref[...]))
    h = _dot_hp(hdn, w4_ref[...]) * jnp.exp(-t_ref[...] * dl_ref[...])
    h_fwd, h_bwd = h[:, :HY_WIDTH], h[:, HY_WIDTH:]
    row = lax.broadcasted_iota(jnp.int32, h_bwd.shape, 0)
    h_bwd = jnp.where(row == 0, 0.0, h_bwd)
    norm = jnp.sum(jnp.abs(h_fwd), 0, keepdims=True) + jnp.sum(jnp.abs(h_bwd), 0, keepdims=True)
    inv_ref[...] = 1.0 / norm
    taps_ref[0] = h_fwd
    taps_ref[1] = h_bwd


def _hy_filter(seq, w1, b1, w2, b2, w3, b3, w4, freq):
    t = jnp.linspace(0.0, 1.0, seq, dtype=F32)[:, None]
    w = 2.0 * math.pi * jnp.arange(seq, dtype=F32)[:, None] / seq
    f = jnp.linspace(1e-4, HY_BANDS - 1, HY_BANDS, dtype=F32)[None, :]
    z = jnp.concatenate([t, jnp.cos(f * w), -jnp.sin(f * w)], -1)
    z = jnp.pad(z, ((0, 0), (0, HY_ORDER - HY_EMB)))
    w1p = jnp.pad(w1, ((0, HY_ORDER - HY_EMB), (0, 0)))
    deltas = jnp.abs(jnp.linspace(math.log(HY_TARGET) / HY_SLOW, math.log(HY_TARGET) / HY_FAST, HY_WIDTH, dtype=F32))
    row = lambda v: v.reshape(1, -1)
    return pl.pallas_call(
        _hy_filter_kernel,
        out_shape=[jax.ShapeDtypeStruct((2, seq, HY_WIDTH), F32), jax.ShapeDtypeStruct((1, HY_WIDTH), F32)],
        compiler_params=pltpu.CompilerParams(vmem_limit_bytes=VMEM_LIMIT),
        name="hyena_filter",
    )(z, t, w1p, row(b1), w2, row(b2), w3, row(b3), w4, row(freq), row(jnp.tile(deltas, 2)))


def _hyena(pb, hy, seq):
    conv_w, conv_b, w1, b1, w2, b2, w3, b3, w4, freq, d_skip = hy
    taps, inv_norm = _hy_filter(seq, w1, b1, w2, b2, w3, b3, w4, freq)
    h_spec = _hy_filter_spec(_hy_stage1(taps, seq, False), inv_norm, seq)
    u, x0c = _hy_prep(pb, conv_w, conv_b)
    y = _hy_out(_hy_mid(_hy_stage1(u, seq, True), h_spec, seq), seq)
    flat = lambda a: a.reshape(-1, HY_WIDTH)
    return flat(y), flat(u), flat(x0c), d_skip.reshape(1, HY_WIDTH)


def _take_cols(w, idx):
    idx = np.asarray(idx)
    return jnp.where(jnp.asarray(idx >= 0)[None, :], w[:, np.maximum(idx, 0)], 0.0)


def _mix_weights(w_in, w_out, a_qn, a_kn, q_g, kv_g, w_uq, w_ukv):
    col = np.arange
    q_cols = np.concatenate([col(h * HEAD_DIM, (h + 1) * HEAD_DIM) for h in A_HEAD_ORDER])
    pad = lambda n: -np.ones(n, np.int64)
    in_idx = np.concatenate([q_cols, col(A_WIDTH, OFF_KR), pad(HEAD_DIM), col(OFF_KR, OFF_KR + C_ROPE),
                             pad(LANES - HEAD_DIM - C_ROPE)])
    dq = C_NOPE + C_ROPE
    uq_idx = np.concatenate([np.concatenate([col(h * dq, (h + 1) * dq), pad(LANES - dq)]) for h in range(C_HEADS)])
    dkv = C_NOPE + C_V
    uk_idx = np.concatenate([np.concatenate([col(h * dkv, h * dkv + C_NOPE), pad(LANES - C_NOPE)])
                             for h in range(C_HEADS)])
    v_cols = lambda h: col(h * dkv + C_NOPE, (h + 1) * dkv)
    uv_idx = np.concatenate([np.concatenate([v_cols(h), pad(LANES - C_V)] if h % 2 == 0 else
                                            [pad(LANES - C_V), v_cols(h)]) for h in range(C_HEADS)])
    half = np.arange(LANES) // HEAD_DIM
    e = (half[:, None] == half[None, :]).astype(np.float32)
    tile2 = lambda g: jnp.tile(g, LANES // HEAD_DIM).reshape(1, LANES)
    return dict(
        w_in=_take_cols(w_in, in_idx).astype(BF16),
        e=jnp.asarray(e, BF16),
        gq=tile2(a_qn), gk=tile2(a_kn),
        cq_g=q_g.reshape(1, -1), ckv_g=kv_g.reshape(1, -1),
        w_uq=_take_cols(w_uq, uq_idx).astype(BF16),
        w_uk=_take_cols(w_ukv, uk_idx).astype(BF16),
        w_uv=_take_cols(w_ukv, uv_idx).astype(BF16),
        wa=w_out[q_cols].astype(BF16),
        wb=w_out[A_WIDTH:A_WIDTH + HY_WIDTH].astype(BF16),
        wc=w_out[A_WIDTH + HY_WIDTH:].astype(BF16),
    )


def _rope_tables(seq):
    rows = seq // GRID_W
    row = jnp.repeat(jnp.arange(rows, dtype=F32), GRID_W)
    colv = jnp.tile(jnp.arange(GRID_W, dtype=F32), rows)

    def cos_sin(rot_dim):
        n_freq = rot_dim // 4
        inv = ROPE_THETA ** (-jnp.arange(n_freq, dtype=F32) / n_freq)
        ang = jnp.concatenate([row[:, None] * inv, colv[:, None] * inv], -1)
        return jnp.cos(ang), jnp.sin(ang)

    cos, sin = cos_sin(HEAD_DIM)
    cos_a = jnp.tile(cos, (1, 2 * LANES // HEAD_DIM))
    sin_a = jnp.tile(jnp.concatenate([-sin, sin], -1), (1, LANES // HEAD_DIM))
    cos, sin = cos_sin(C_ROPE)
    ones = jnp.ones((seq, C_NOPE), F32)
    tail = LANES - C_NOPE - C_ROPE
    cos_c = jnp.concatenate([ones, cos, cos, jnp.ones((seq, tail), F32)], -1)
    sin_c = jnp.concatenate([0.0 * ones, -sin, sin, jnp.zeros((seq, tail), F32)], -1)
    return cos_a, sin_a, cos_c, sin_c


def kernel(x, c, ctx, c_ctx, ada_w, ada_b, ffn1_w_gu, ffn1_w_down, ffn2_w_gu, ffn2_w_down, ln_g, ln_b, w_in, w_out,
           a_q_norm, a_k_norm, hy_conv_w, hy_conv_b, hy_f_w1, hy_f_b1, hy_f_w2, hy_f_b2, hy_f_w3, hy_f_b3, hy_f_w4,
           hy_f_freq, hy_bias, mla_q_norm, mla_kv_norm, mla_w_uq, mla_w_ukv):
    b, seq, d = x.shape
    n_ctx = ctx.shape[1]
    depth = ada_w.shape[0]
    assert b % 2 == 0 and seq % GRID_W == 0

    tm = min(512, seq)
    tm_ctx = min(512, b * n_ctx)
    tiles_per_seq = seq // tm
    lat_batch = lambda i: i // tiles_per_seq
    tm_mix = min(1024, seq)
    mix_tiles_per_seq = seq // tm_mix
    mix_batch = lambda i: i // mix_tiles_per_seq
    ctx_batch = lambda i: b

    mod_rows = ((b + 1 + 7) // 8) * 8
    c_all = jnp.concatenate([c, c_ctx[None], jnp.zeros((mod_rows - b - 1, d), F32)], 0)
    mods = _ada_mod(c_all, ada_w, ada_b).reshape(depth, mod_rows, N_MOD // 3, 3, d)

    rope_tabs = _rope_tables(seq)
    x2 = x.reshape(b * seq, d)
    ctx2 = ctx.reshape(b * n_ctx, d)

    for l in range(depth):
        need_ctx = l < depth - 1
        m4 = mods[l]
        ffn1 = (ffn1_w_gu[l, :, :D_FF].astype(BF16), ffn1_w_gu[l, :, D_FF:].astype(BF16),
                ffn1_w_down[l].astype(BF16))
        ffn2 = (ffn2_w_gu[l, :, :D_FF].astype(BF16), ffn2_w_gu[l, :, D_FF:].astype(BF16),
                ffn2_w_down[l].astype(BF16))
        wts = _mix_weights(w_in[l], w_out[l], a_q_norm[l], a_k_norm[l], mla_q_norm[l], mla_kv_norm[l],
                           mla_w_uq[l], mla_w_ukv[l])
        hy = (hy_conv_w[l], hy_conv_b[l], hy_f_w1[l], hy_f_b1[l], hy_f_w2[l], hy_f_b2[l], hy_f_w3[l], hy_f_b3[l],
              hy_f_w4[l], hy_f_freq[l], hy_bias[l])

        x2 = _ffn_sublayer(x2, m4, 0, lat_batch, *ffn1, ln_g[l, 0], ln_b[l, 0], tm)
        ctx2 = _ffn_sublayer(ctx2, m4, 0, ctx_batch, *ffn1, ln_g[l, 0], ln_b[l, 0], tm_ctx)

        qa, ka, va, pb, qc, kc, vc = _mix_in(x2, m4, mix_batch, wts, rope_tabs, mix_tiles_per_seq, tm_mix)
        qa_c, ka_c, va_c, pb_c, qc_c, kc_c, vc_c = _mix_in(ctx2, m4, ctx_batch, wts, None, 1, tm_ctx)
        r3 = lambda a, n: a.reshape(b, n, a.shape[-1])
        ka_c, va_c, kc_c, vc_c = (r3(a, n_ctx) for a in (ka_c, va_c, kc_c, vc_c))

        oa = _gqa_attention(r3(qa, seq), [(ka_c, va_c), (r3(ka, seq), r3(va, seq))], *ATTN_TILES_A)
        oc = _mla_attention(r3(qc, seq), [(kc_c, vc_c), (r3(kc, seq), r3(vc, seq))], *ATTN_TILES_C)
        hy_parts = _hyena(r3(pb, seq), hy, seq)
        flat = lambda a: a.reshape(-1, a.shape[-1])
        out_w = (wts["wa"], wts["wb"], wts["wc"], ln_g[l, 1], ln_b[l, 1])
        if need_ctx:
            oa_c = _gqa_attention(r3(qa_c, n_ctx), [(ka_c, va_c)], *ATTN_TILES_A)
            oc_c = _mla_attention(r3(qc_c, n_ctx), [(kc_c, vc_c)], *ATTN_TILES_C)
            hy_parts_c = _hyena(r3(pb_c, n_ctx), hy, n_ctx)
            ctx2 = _ffn_sublayer(ctx2, m4, 2, ctx_batch, *ffn2, ln_g[l, 2], ln_b[l, 2], tm_ctx,
                                 mix=(flat(oa_c), hy_parts_c, flat(oc_c), *out_w))
        x2 = _ffn_sublayer(x2, m4, 2, lat_batch, *ffn2, ln_g[l, 2], ln_b[l, 2], tm,
                           mix=(flat(oa), hy_parts, flat(oc), *out_w))
    return x2.reshape(b, seq, d)
```

```python
import functools
import math

import numpy as np
import jax
import jax.numpy as jnp
from jax import lax
from jax.experimental import pallas as pl
from jax.experimental.pallas import tpu as pltpu

F32 = jnp.float32
BF16 = jnp.bfloat16

D_MODEL = 1024
GRID_W = 64
N_MOD = 9
D_FF = 2816
DEPTH = 2
ALPHA = (2 * DEPTH) ** 0.25
EPS = 1e-6
A_HEADS = 6
A_KV_HEADS = 2
HEAD_DIM = 64
ROPE_THETA = 10000.0
HY_WIDTH = 256
HY_EMB = 33
HY_BANDS = (HY_EMB - 1) // 2
HY_ORDER = 64
HY_TARGET = 1e-2
HY_FAST = 0.3
HY_SLOW = 1.5
C_HEADS = 6
C_Q_LORA = 256
C_KV_LORA = 128
C_NOPE = 64
C_ROPE = 32
C_V = 64
A_WIDTH = A_HEADS * HEAD_DIM
C_WIDTH = C_HEADS * C_V
A_COLS = (A_HEADS + 2 * A_KV_HEADS) * HEAD_DIM
B_COLS = 3 * HY_WIDTH
C_COLS = C_Q_LORA + C_KV_LORA + C_ROPE

LANES = 128
MXU_DIM = 256
VMEM_LIMIT = 56 * 1024 * 1024

P_COLS = A_COLS + B_COLS + C_Q_LORA + C_KV_LORA + LANES
OFF_K = A_WIDTH
OFF_V = A_WIDTH + A_KV_HEADS * HEAD_DIM
OFF_B = A_COLS
OFF_CQ = A_COLS + B_COLS
OFF_CKV = OFF_CQ + C_Q_LORA
OFF_KR = OFF_CKV + C_KV_LORA
A_HEAD_ORDER = (0, 3, 1, 4, 2, 5)
CQ_PAD = C_HEADS * LANES


def _cparams(sem):
    return pltpu.CompilerParams(dimension_semantics=sem, vmem_limit_bytes=VMEM_LIMIT)


def _layer_norm(x):
    mu = jnp.mean(x, -1, keepdims=True)
    xc = x - mu
    var = jnp.mean(xc * xc, -1, keepdims=True)
    return xc * lax.rsqrt(var + EPS)


def _silu(x):
    return x * jax.nn.sigmoid(x)


def _dot(a, b):
    return jnp.dot(a, b, preferred_element_type=F32)


def _dot_nt(a, b):
    return lax.dot_general(a, b, (((1,), (1,)), ((), ())), preferred_element_type=F32)


def _dot_hp(a, b):
    return jnp.dot(a, b, preferred_element_type=F32, precision=lax.Precision.HIGHEST)


def _ada_kernel(c_ref, w_ref, b_ref, o_ref):
    a = _silu(c_ref[...]).astype(BF16)
    o_ref[0] = _dot(a, w_ref[0].astype(BF16)) + b_ref[0]


def _ada_mod(c_all, ada_w, ada_b):
    depth, d, n = ada_w.shape
    rows = c_all.shape[0]
    tn = n // 8
    return pl.pallas_call(
        _ada_kernel,
        grid=(depth, n // tn),
        in_specs=[
            pl.BlockSpec((rows, d), lambda l, j: (0, 0)),
            pl.BlockSpec((1, d, tn), lambda l, j: (l, 0, j)),
            pl.BlockSpec((1, 1, tn), lambda l, j: (l, 0, j)),
        ],
        out_specs=pl.BlockSpec((1, rows, tn), lambda l, j: (l, 0, j)),
        out_shape=jax.ShapeDtypeStruct((depth, rows, n), F32),
        compiler_params=_cparams(("arbitrary", "arbitrary")),
        name="ada_mod",
    )(c_all, ada_w, ada_b.reshape(depth, 1, n))


def _ffn_kernel(*refs, bounds, fused_mix):
    if fused_mix:
        (xin_ref, modm_ref, oa_ref, y_ref, u_ref, x0_ref, d_ref, oc_ref, wa_ref, wb_ref, wc_ref, g1_ref, b1_ref,
         mod_ref, wgu_ref, wd_ref, g_ref, b_ref, o_ref, x_ref) = refs
    else:
        x_ref, mod_ref, wgu_ref, wd_ref, g_ref, b_ref, o_ref = refs
    f = wd_ref.shape[0]
    mod = mod_ref[0, 0]
    half = x_ref.shape[0] // 2
    halves = (slice(0, half), slice(half, 2 * half))
    n_chunks = len(bounds) - 1

    def mix_rows(r):
        ob = (y_ref[r, :].astype(F32) + u_ref[r, :].astype(F32) * d_ref[...]) * x0_ref[r, :].astype(F32)
        mix = (_dot(oa_ref[r, :], wa_ref[...]) + _dot(ob.astype(BF16), wb_ref[...])
               + _dot(oc_ref[r, :], wc_ref[...]))
        y = ALPHA * xin_ref[r, :] + modm_ref[0, 0][2:3] * mix
        x_ref[r, :] = _layer_norm(y) * g1_ref[...] + b1_ref[...]

    hs = []
    for r in halves:
        if fused_mix:
            mix_rows(r)
        hs.append((_layer_norm(x_ref[r, :]) * (1.0 + mod[1:2]) + mod[0:1]).astype(BF16))
    acc = None
    for k, (lo, hi) in enumerate(zip(bounds[:-1], bounds[1:])):
        if k == 0:
            a = jnp.concatenate([(_silu(_dot(hr, wgu_ref[:, lo:hi])) * _dot(hr, wgu_ref[:, f + lo:f + hi])).astype(BF16)
                                 for hr in hs], axis=0)
            h = jnp.concatenate(hs, axis=0)
        else:
            a = (_silu(_dot(h, wgu_ref[:, lo:hi])) * _dot(h, wgu_ref[:, f + lo:f + hi])).astype(BF16)
        if k == n_chunks - 1:
            for r in halves:
                part = _dot(a[r, :], wd_ref[lo:hi, :])
                y = ALPHA * x_ref[r, :] + (0.5 * mod[2:3]) * (part if acc is None else acc[r, :] + part)
                o_ref[r, :] = _layer_norm(y) * g_ref[...] + b_ref[...]
        else:
            part = _dot(a, wd_ref[lo:hi, :])
            acc = part if acc is None else acc + part


def _ffn_sublayer(x2, mods4, mod_group, batch_of_tile, wgu, wd, g, b, tm, mix=None):
    rows, d = x2.shape
    f = wd.shape[0]
    const = lambda i: (0, 0)
    row_spec = lambda w: pl.BlockSpec((tm, w), lambda i: (i, 0))
    mod_spec = lambda group: pl.BlockSpec((1, 1, 3, d), lambda i: (batch_of_tile(i), group, 0, 0))
    vec_spec = pl.BlockSpec((1, d), const)
    n_tiles = -(-f // MXU_DIM)
    bounds = (0, min(f, ((n_tiles + 1) // 2) * MXU_DIM), f)
    in_specs, args, scratch = [row_spec(d)], [x2], []
    if mix is not None:
        oa, (y, u, x0c, d_skip), oc, wa, wb, wc, g1, b1 = mix
        in_specs += [mod_spec(1), row_spec(A_WIDTH), row_spec(HY_WIDTH), row_spec(HY_WIDTH), row_spec(HY_WIDTH),
                     pl.BlockSpec((1, HY_WIDTH), const), row_spec(C_WIDTH),
                     pl.BlockSpec((A_WIDTH, d), const), pl.BlockSpec((HY_WIDTH, d), const),
                     pl.BlockSpec((C_WIDTH, d), const), vec_spec, vec_spec]
        args += [mods4, oa, y, u, x0c, d_skip, oc, wa, wb, wc, g1.reshape(1, d), b1.reshape(1, d)]
        scratch = [pltpu.VMEM((tm, d), F32)]
    in_specs += [mod_spec(mod_group),
                 pl.BlockSpec((d, 2 * f), const, pipeline_mode=pl.Buffered(1)),
                 pl.BlockSpec((f, d), const, pipeline_mode=pl.Buffered(1)),
                 vec_spec, vec_spec]
    args += [mods4, wgu, wd, g.reshape(1, d), b.reshape(1, d)]
    return pl.pallas_call(
        functools.partial(_ffn_kernel, bounds=bounds, fused_mix=mix is not None),
        grid=(rows // tm,),
        in_specs=in_specs,
        out_specs=row_spec(d),
        out_shape=jax.ShapeDtypeStruct((rows, d), F32),
        scratch_shapes=scratch,
        compiler_params=_cparams(("arbitrary",)),
        name="mix_out_ffn" if mix is not None else "ffn_sublayer",
    )(*args)


def _rope(x, cos, sin_signed, half):
    lane = lax.broadcasted_iota(jnp.int32, x.shape, 1)
    first = (lane % (2 * half)) < half
    partner = jnp.where(first, pltpu.roll(x, LANES - half, 1), pltpu.roll(x, half, 1))
    return x * cos + partner * sin_signed


def _half_tile_mean_sq(x, e_ref):
    return _dot((x * x).astype(BF16), e_ref[...]) * (1.0 / HEAD_DIM)


def _mix_in_kernel(x_ref, mod_ref, w_ref, e_ref, gq_ref, gk_ref, cq_g_ref, ckv_g_ref, wuq_ref, wuk_ref, wuv_ref,
                   *rest, use_rope):
    if use_rope:
        cos_a_ref, sin_a_ref, cos_c_ref, sin_c_ref = rest[:4]
        rest = rest[4:]
    qa_ref, ka_ref, va_ref, pb_ref, qc_ref, kc_ref, vc_ref = rest

    mod = mod_ref[0, 0]
    half = x_ref.shape[0] // 2
    scale_c = LOG2E * (C_NOPE + C_ROPE) ** -0.5
    for r in (slice(0, half), slice(half, 2 * half)):
        h = (_layer_norm(x_ref[r, :]) * (1.0 + mod[1:2]) + mod[0:1]).astype(BF16)
        p = _dot(h, w_ref[...])

        def rope_a(t):
            return _rope(t, cos_a_ref[r, :], sin_a_ref[r, :], HEAD_DIM // 2) if use_rope else t

        def rope_c(t):
            return _rope(t, cos_c_ref[r, :], sin_c_ref[r, :], C_ROPE // 2) if use_rope else t

        for c in range(A_WIDTH // LANES):
            t = p[:, c * LANES:(c + 1) * LANES]
            t = t * lax.rsqrt(_half_tile_mean_sq(t, e_ref) + EPS) * gq_ref[...]
            qa_ref[r, c * LANES:(c + 1) * LANES] = (rope_a(t) * (LOG2E * HEAD_DIM ** -0.5)).astype(BF16)
        t = p[:, OFF_K:OFF_K + LANES]
        t = t * lax.rsqrt(_half_tile_mean_sq(t, e_ref) + EPS) * gk_ref[...]
        ka_ref[r, :] = rope_a(t).astype(BF16)
        vt = p[:, OFF_V:OFF_V + LANES].T
        ones = jnp.ones((VT_ROWS - HEAD_DIM, half), BF16)
        for g in range(A_KV_HEADS):
            va_ref[g * VT_ROWS:g * VT_ROWS + HEAD_DIM, r] = vt[g * HEAD_DIM:(g + 1) * HEAD_DIM].astype(BF16)
            va_ref[g * VT_ROWS + HEAD_DIM:(g + 1) * VT_ROWS, r] = ones

        pb_ref[r, :] = p[:, OFF_B:OFF_B + B_COLS]

        c_q = p[:, OFF_CQ:OFF_CQ + C_Q_LORA]
        c_q = c_q * lax.rsqrt(jnp.mean(c_q * c_q, -1, keepdims=True) + EPS) * cq_g_ref[...]
        qq = _dot(c_q.astype(BF16), wuq_ref[...])
        c_kv = p[:, OFF_CKV:OFF_CKV + C_KV_LORA]
        c_kv = (c_kv * lax.rsqrt(jnp.mean(c_kv * c_kv, -1, keepdims=True) + EPS) * ckv_g_ref[...]).astype(BF16)
        kn = _dot(c_kv, wuk_ref[...])
        vvt = _dot_nt(wuv_ref[...], c_kv)
        ones_row = lax.broadcasted_iota(jnp.int32, vvt.shape, 0) % VT_ROWS >= C_V
        vc_ref[:, r] = jnp.where(ones_row, 1.0, vvt).astype(BF16)
        k_r = rope_c(p[:, OFF_KR:OFF_KR + LANES])
        for hd in range(C_HEADS):
            sl = slice(hd * LANES, (hd + 1) * LANES)
            qc_ref[r, sl] = (rope_c(qq[:, sl]) * scale_c).astype(BF16)
            kc_ref[r, sl] = (kn[:, sl] + k_r).astype(BF16)


def _mix_in(x2, mods4, batch_of_tile, wts, rope_tabs, tiles_per_seq, tm):
    rows, d = x2.shape
    const = lambda i: (0, 0)
    use_rope = rope_tabs is not None
    in_specs = [
        pl.BlockSpec((tm, d), lambda i: (i, 0)),
        pl.BlockSpec((1, 1, 3, d), lambda i: (batch_of_tile(i), 1, 0, 0)),
        pl.BlockSpec((d, P_COLS), const, pipeline_mode=pl.Buffered(1)),
        pl.BlockSpec((LANES, LANES), const),
        pl.BlockSpec((1, LANES), const),
        pl.BlockSpec((1, LANES), const),
        pl.BlockSpec((1, C_Q_LORA), const),
        pl.BlockSpec((1, C_KV_LORA), const),
        pl.BlockSpec((C_Q_LORA, CQ_PAD), const),
        pl.BlockSpec((C_KV_LORA, CQ_PAD), const),
        pl.BlockSpec((C_HEADS * VT_ROWS, C_KV_LORA), const),
    ]
    args = [x2, mods4, wts["w_in"], wts["e"], wts["gq"], wts["gk"], wts["cq_g"], wts["ckv_g"],
            wts["w_uq"], wts["w_uk"], wts["w_uv"]]
    if use_rope:
        in_specs += [pl.BlockSpec((tm, LANES), lambda i: (i % tiles_per_seq, 0))] * 4
        args += list(rope_tabs)
    row_out = lambda w, dt: (pl.BlockSpec((tm, w), lambda i: (i, 0)), jax.ShapeDtypeStruct((rows, w), dt))
    col_out = lambda w: (pl.BlockSpec((w, tm), lambda i: (0, i)), jax.ShapeDtypeStruct((w, rows), BF16))
    outs = [row_out(A_WIDTH, BF16), row_out(LANES, BF16), col_out(A_KV_HEADS * VT_ROWS), row_out(B_COLS, F32),
            row_out(CQ_PAD, BF16), row_out(CQ_PAD, BF16), col_out(C_HEADS * VT_ROWS)]
    return pl.pallas_call(
        functools.partial(_mix_in_kernel, use_rope=use_rope),
        grid=(rows // tm,),
        in_specs=in_specs,
        out_specs=[spec for spec, _ in outs],
        out_shape=[shape for _, shape in outs],
        compiler_params=_cparams(("arbitrary",)),
        name="mix_in_rope" if use_rope else "mix_in",
    )(*args)


ATTN_TILES_A = (256, 512, 2)
ATTN_TILES_C = (512, 512)
LOG2E = math.log2(math.e)


def _key_chunks(kv_refs, tk):
    chunks = []
    for s in range(len(kv_refs) // 2):
        n_rows = kv_refs[2 * s].shape[1]
        size = min(tk, n_rows)
        chunks += [(s, off, size) for off in range(0, n_rows, size)]
    return chunks


VT_ROWS = HEAD_DIM + 16


def _flash(qt_get, n_chain, chunks, k_get, vt_get, scratch):
    s_refs = [scratch[2 * c:2 * c + 2] for c in range(n_chain)]
    m_refs = scratch[2 * n_chain:3 * n_chain]
    acc_refs = scratch[3 * n_chain:4 * n_chain]

    def scores(i):
        size = chunks[i][2]
        for c in range(n_chain):
            s_refs[c][i % 2][0:size, :] = _dot(k_get(c, chunks[i]), qt_get(c))

    def softmax_pv(i):
        size = chunks[i][2]
        blocks = [slice(lo, min(lo + MXU_DIM, size)) for lo in range(0, size, MXU_DIM)]
        for c in range(n_chain):
            s_ref = s_refs[c][i % 2]
            m_new = functools.reduce(jnp.maximum, [jnp.max(s_ref[rows, :], axis=0, keepdims=True) for rows in blocks])
            if i > 0:
                m_old = m_refs[c][...]
                m_new = jnp.maximum(m_old, m_new)
            vt = vt_get(c, chunks[i])
            pv = functools.reduce(jnp.add, [_dot(vt[:, rows], jnp.exp2(s_ref[rows, :] - m_new).astype(BF16))
                                            for rows in blocks])
            acc_refs[c][...] = pv if i == 0 else jnp.exp2(m_old - m_new) * acc_refs[c][...] + pv
            m_refs[c][...] = m_new

    scores(0)
    for i in range(len(chunks)):
        if i + 1 < len(chunks):
            scores(i + 1)
        softmax_pv(i)
    return acc_refs


def _attn_scratch(n_chain, n_q, tk):
    return ([pltpu.VMEM((tk, n_q), F32)] * (2 * n_chain) + [pltpu.VMEM((1, n_q), F32)] * n_chain
            + [pltpu.VMEM((VT_ROWS, n_q), F32)] * n_chain)


def _normalised_t(acc_ref, cols):
    return acc_ref[0:HEAD_DIM, cols] / acc_ref[HEAD_DIM:HEAD_DIM + 1, cols]


def _gqa_kernel(q_ref, *refs, n_src, tk, n_sub):
    kv_refs, o_ref = refs[:2 * n_src], refs[2 * n_src]
    scratch = refs[2 * n_src + 1:]
    per_sub = len(scratch) // n_sub
    tq = q_ref.shape[1] // n_sub
    n_tiles = A_WIDTH // LANES
    low = lax.broadcasted_iota(jnp.int32, (LANES, tq), 0) < HEAD_DIM
    k_get = lambda g, ch: kv_refs[2 * ch[0]][0, ch[1]:ch[1] + ch[2], :]
    vt_get = lambda g, ch: kv_refs[2 * ch[0] + 1][g * VT_ROWS:(g + 1) * VT_ROWS, ch[1]:ch[1] + ch[2]]
    for t in range(n_sub):
        qt_ref, sub_scratch = scratch[t * per_sub], scratch[t * per_sub + 1:(t + 1) * per_sub]
        q_rows = slice(t * tq, (t + 1) * tq)
        for c in range(n_tiles):
            qt = q_ref[0, q_rows, c * LANES:(c + 1) * LANES].astype(F32).T
            for g in range(A_KV_HEADS):
                qt_ref[g, :, c * tq:(c + 1) * tq] = jnp.where(low == (g == 0), qt, 0.0).astype(BF16)
        qt_get = lambda g, qt_ref=qt_ref: qt_ref[g]
        acc = _flash(qt_get, A_KV_HEADS, _key_chunks(kv_refs, tk), k_get, vt_get, sub_scratch)
        for c in range(n_tiles):
            cols = slice(c * tq, (c + 1) * tq)
            ot = jnp.concatenate([_normalised_t(acc[g], cols) for g in range(A_KV_HEADS)], axis=0)
            o_ref[0, q_rows, c * LANES:(c + 1) * LANES] = ot.T.astype(BF16)


def _gqa_attention(q, kv_list, tq, tk, n_sub):
    b, lq, _ = q.shape
    tk = min(tk, max(k.shape[1] for k, _ in kv_list))
    if lq < tq * n_sub:
        tq, n_sub = min(tq, lq), 1
    n_q = (A_HEADS // A_KV_HEADS) * tq
    tq = tq * n_sub
    in_specs = [pl.BlockSpec((1, tq, A_WIDTH), lambda i, j: (i, j, 0))]
    args = [q]
    for k, vt in kv_list:
        in_specs.append(pl.BlockSpec((1, k.shape[1], LANES), lambda i, j: (i, 0, 0)))
        in_specs.append(pl.BlockSpec((vt.shape[0], k.shape[1]), lambda i, j: (0, i)))
        args += [k, vt]
    return pl.pallas_call(
        functools.partial(_gqa_kernel, n_src=len(kv_list), tk=tk, n_sub=n_sub),
        grid=(b, lq // tq),
        in_specs=in_specs,
        out_specs=pl.BlockSpec((1, tq, A_WIDTH), lambda i, j: (i, j, 0)),
        out_shape=jax.ShapeDtypeStruct((b, lq, A_WIDTH), BF16),
        scratch_shapes=([pltpu.VMEM((A_KV_HEADS, LANES, n_q), BF16)]
                        + _attn_scratch(A_KV_HEADS, n_q, tk)) * n_sub,
        compiler_params=_cparams(("arbitrary", "arbitrary")),
        name="gqa_attention",
    )(*args)


def _mla_kernel(q_ref, *refs, n_src, tk):
    kv_refs, o_ref = refs[:2 * n_src], refs[2 * n_src]
    scratch = refs[2 * n_src + 1:]
    n_pairs = C_HEADS // 2
    per_pair = len(scratch) // n_pairs
    for hp in range(n_pairs):
        qt_ref, pair_scratch = scratch[hp * per_pair], scratch[hp * per_pair + 1:(hp + 1) * per_pair]
        lanes = lambda e, hp=hp: slice((2 * hp + e) * LANES, (2 * hp + e + 1) * LANES)
        for e in range(2):
            qt_ref[e] = q_ref[0, :, lanes(e)].astype(F32).T.astype(BF16)
        qt_get = lambda e, qt_ref=qt_ref: qt_ref[e]
        k_get = lambda e, ch, lanes=lanes: kv_refs[2 * ch[0]][0, ch[1]:ch[1] + ch[2], lanes(e)]
        vt_get = lambda e, ch, hp=hp: kv_refs[2 * ch[0] + 1][(2 * hp + e) * VT_ROWS:(2 * hp + e + 1) * VT_ROWS,
                                                            ch[1]:ch[1] + ch[2]]
        acc = _flash(qt_get, 2, _key_chunks(kv_refs, tk), k_get, vt_get, pair_scratch)
        ot = jnp.concatenate([_normalised_t(acc[e], slice(None)) for e in range(2)], axis=0)
        o_ref[0, :, hp * LANES:(hp + 1) * LANES] = ot.T.astype(BF16)


def _mla_attention(q, kv_list, tq, tk):
    b, lq, _ = q.shape
    tq, tk = min(tq, lq), min(tk, max(k.shape[1] for k, _ in kv_list))
    in_specs = [pl.BlockSpec((1, tq, CQ_PAD), lambda i, j: (i, j, 0))]
    args = [q]
    for k, vt in kv_list:
        in_specs.append(pl.BlockSpec((1, k.shape[1], CQ_PAD), lambda i, j: (i, 0, 0)))
        in_specs.append(pl.BlockSpec((vt.shape[0], k.shape[1]), lambda i, j: (0, i)))
        args += [k, vt]
    return pl.pallas_call(
        functools.partial(_mla_kernel, n_src=len(kv_list), tk=tk),
        grid=(b, lq // tq),
        in_specs=in_specs,
        out_specs=pl.BlockSpec((1, tq, C_WIDTH), lambda i, j: (i, j, 0)),
        out_shape=jax.ShapeDtypeStruct((b, lq, C_WIDTH), BF16),
        scratch_shapes=([pltpu.VMEM((2, LANES, tq), BF16)] + _attn_scratch(2, tq, tk)) * (C_HEADS // 2),
        compiler_params=_cparams(("arbitrary", "arbitrary")),
        name="mla_attention",
    )(*args)


def _fft_dims(seq):
    n = 2 * seq
    n2 = min(LANES, n // 16)
    return n // n2, n2


def _stack_real(m):
    return np.block([[m.real, -m.imag], [m.imag, m.real]]).astype(np.float32)


@functools.lru_cache(maxsize=None)
def _fft_constants(seq):
    n1, n2 = _fft_dims(seq)
    n = n1 * n2
    k1 = np.arange(n1)[:, None]
    f1 = np.exp(-2j * np.pi * k1 * np.arange(n1 // 2)[None, :] / n1)
    f2 = np.exp(-2j * np.pi * np.arange(n2)[:, None] * np.arange(n2)[None, :] / n2)
    tw = np.exp(-2j * np.pi * k1 * np.arange(n2)[None, :] / n)
    g1 = np.conj(f1).T / n
    return dict(
        m1=_stack_real(f1), m2=_stack_real(f2), m2_inv=_stack_real(np.conj(f2)), m1_inv=_stack_real(g1),
        tw_r=tw.real.astype(np.float32)[:, :, None], tw_i=tw.imag.astype(np.float32)[:, :, None])


def _hy_prep_kernel(v_ref, x1_ref, x0_ref, wv_ref, wx1_ref, wx0_ref, bv_ref, bx1_ref, bx0_ref, u_ref, x0c_ref):
    seq = v_ref.shape[1]
    row = lax.broadcasted_iota(jnp.int32, (seq, LANES), 0)
    first, last = row == 0, row == seq - 1

    def conv(p_ref, w_ref, b_ref):
        t = p_ref[0]
        prev = jnp.where(first, 0.0, pltpu.roll(t, 1, 0))
        nxt = jnp.where(last, 0.0, pltpu.roll(t, seq - 1, 0))
        return prev * w_ref[0:1] + t * w_ref[1:2] + nxt * w_ref[2:3] + b_ref[...]

    u_ref[0] = (conv(v_ref, wv_ref, bv_ref) * conv(x1_ref, wx1_ref, bx1_ref)).astype(BF16)
    x0c_ref[0] = conv(x0_ref, wx0_ref, bx0_ref).astype(BF16)


def _hy_prep(pb, conv_w, conv_b):
    b, seq, _ = pb.shape
    n_c = HY_WIDTH // LANES
    p_spec = lambda part: pl.BlockSpec((1, seq, LANES), lambda i, c: (i, 0, part * n_c + c))
    w_spec = lambda part: pl.BlockSpec((3, LANES), lambda i, c: (0, part * n_c + c))
    b_spec = lambda part: pl.BlockSpec((1, LANES), lambda i, c: (0, part * n_c + c))
    out_spec = pl.BlockSpec((1, seq, LANES), lambda i, c: (i, 0, c))
    cb = conv_b.reshape(1, B_COLS)
    return pl.pallas_call(
        _hy_prep_kernel,
        grid=(b, n_c),
        in_specs=[p_spec(0), p_spec(1), p_spec(2), w_spec(0), w_spec(1), w_spec(2), b_spec(0), b_spec(1), b_spec(2)],
        out_specs=[out_spec, out_spec],
        out_shape=[jax.ShapeDtypeStruct((b, seq, HY_WIDTH), BF16)] * 2,
        compiler_params=_cparams(("arbitrary", "arbitrary")),
        name="hyena_prep",
    )(pb, pb, pb, conv_w, conv_w, conv_w, cb, cb, cb)


def _hy_stage1_kernel(u_ref, m1_ref, a_ref):
    members, h, lc = u_ref.shape
    z = u_ref[...].astype(F32).reshape(members * h, lc).astype(BF16)
    a = _dot(m1_ref[...].astype(BF16), z)
    a_ref[0] = a.reshape(2, a.shape[0] // 2, lc).astype(BF16)


def _hy_stage1(u, seq, paired):
    b, _, ch = u.shape
    n1, n2 = _fft_dims(seq)
    cols = n2 * ch
    lc = min(cols, 4096)
    members = 2 if paired else 1
    m1 = _fft_constants(seq)["m1"]
    m1 = jnp.asarray(m1 if paired else m1[:, :n1 // 2])
    return pl.pallas_call(
        _hy_stage1_kernel,
        grid=(b // members, cols // lc),
        in_specs=[pl.BlockSpec((members, n1 // 2, lc), lambda j, c: (j, 0, c)),
                  pl.BlockSpec(m1.shape, lambda j, c: (0, 0))],
        out_specs=pl.BlockSpec((1, 2, n1, lc), lambda j, c: (j, 0, 0, c)),
        out_shape=jax.ShapeDtypeStruct((b // members, 2, n1, cols), BF16),
        compiler_params=_cparams(("arbitrary", "arbitrary")),
        name="hyena_stage1",
    )(u.reshape(b, n1 // 2, cols), m1)


def _twiddle(ar, ai, tr, ti):
    return ar * tr - ai * ti, ar * ti + ai * tr


HY_PLANES = 8


def _hy_filter_spec_kernel(a_ref, twr_ref, twi_ref, m2_ref, inv_ref, h_ref):
    n2 = a_ref.shape[3]
    m2 = m2_ref[...].astype(BF16)
    for p in range(a_ref.shape[2]):
        tr, ti = twr_ref[p], twi_ref[p]

        def fwd(seq_idx):
            br, bi = _twiddle(a_ref[seq_idx, 0, p].astype(F32), a_ref[seq_idx, 1, p].astype(F32), tr, ti)
            return _dot(m2, jnp.concatenate([br, bi], axis=0).astype(BF16))

        zf, zb = fwd(0), fwd(1)
        h_ref[p, 0:n2] = (zf[0:n2] + zb[0:n2]) * inv_ref[...]
        h_ref[p, n2:2 * n2] = (zf[n2:2 * n2] - zb[n2:2 * n2]) * inv_ref[...]


def _hy_filter_spec(a, inv_norm, seq):
    n1, n2 = _fft_dims(seq)
    cst = _fft_constants(seq)
    kp = min(HY_PLANES, n1)
    a5 = a.reshape(2, 2, n1, n2, HY_WIDTH)
    return pl.pallas_call(
        _hy_filter_spec_kernel,
        grid=(n1 // kp,),
        in_specs=[pl.BlockSpec((2, 2, kp, n2, HY_WIDTH), lambda k: (0, 0, k, 0, 0)),
                  pl.BlockSpec((kp, n2, 1), lambda k: (k, 0, 0)),
                  pl.BlockSpec((kp, n2, 1), lambda k: (k, 0, 0)),
                  pl.BlockSpec((2 * n2, 2 * n2), lambda k: (0, 0)),
                  pl.BlockSpec((1, HY_WIDTH), lambda k: (0, 0))],
        out_specs=pl.BlockSpec((kp, 2 * n2, HY_WIDTH), lambda k: (k, 0, 0)),
        out_shape=jax.ShapeDtypeStruct((n1, 2 * n2, HY_WIDTH), F32),
        compiler_params=_cparams(("arbitrary",)),
        name="hyena_filter_spectrum",
    )(a5, jnp.asarray(cst["tw_r"]), jnp.asarray(cst["tw_i"]), jnp.asarray(cst["m2"]), inv_norm)


def _hy_mid_kernel(a_ref, twr_ref, twi_ref, m2_ref, m2i_ref, h_ref, q_ref):
    n2 = a_ref.shape[3]
    m2, m2i = m2_ref[...].astype(BF16), m2i_ref[...].astype(BF16)
    for p in range(a_ref.shape[2]):
        tr, ti = twr_ref[p], twi_ref[p]
        br, bi = _twiddle(a_ref[0, 0, p].astype(F32), a_ref[0, 1, p].astype(F32), tr, ti)
        z = _dot(m2, jnp.concatenate([br, bi], axis=0).astype(BF16))
        zr, zi = z[0:n2], z[n2:2 * n2]
        hr, hi = h_ref[p, 0:n2], h_ref[p, n2:2 * n2]
        yr, yi = zr * hr - zi * hi, zr * hi + zi * hr
        y = _dot(m2i, jnp.concatenate([yr, yi], axis=0).astype(BF16))
        qr, qi = _twiddle(y[0:n2], y[n2:2 * n2], tr, -ti)
        q_ref[0, 0, p] = qr.astype(BF16)
        q_ref[0, 1, p] = qi.astype(BF16)


def _hy_mid(a, h_spec, seq):
    pairs = a.shape[0]
    n1, n2 = _fft_dims(seq)
    cst = _fft_constants(seq)
    kp = min(HY_PLANES, n1)
    a5 = a.reshape(pairs, 2, n1, n2, HY_WIDTH)
    blk = pl.BlockSpec((1, 2, kp, n2, HY_WIDTH), lambda k, j: (j, 0, k, 0, 0))
    q = pl.pallas_call(
        _hy_mid_kernel,
        grid=(n1 // kp, pairs),
        in_specs=[blk,
                  pl.BlockSpec((kp, n2, 1), lambda k, j: (k, 0, 0)),
                  pl.BlockSpec((kp, n2, 1), lambda k, j: (k, 0, 0)),
                  pl.BlockSpec((2 * n2, 2 * n2), lambda k, j: (0, 0)),
                  pl.BlockSpec((2 * n2, 2 * n2), lambda k, j: (0, 0)),
                  pl.BlockSpec((kp, 2 * n2, HY_WIDTH), lambda k, j: (k, 0, 0))],
        out_specs=blk,
        out_shape=jax.ShapeDtypeStruct((pairs, 2, n1, n2, HY_WIDTH), BF16),
        compiler_params=_cparams(("arbitrary", "arbitrary")),
        name="hyena_mid",
    )(a5, jnp.asarray(cst["tw_r"]), jnp.asarray(cst["tw_i"]), jnp.asarray(cst["m2"]), jnp.asarray(cst["m2_inv"]),
      h_spec)
    return q.reshape(pairs, 2, n1, n2 * HY_WIDTH)


def _hy_out_kernel(q_ref, m1i_ref, y_ref):
    _, _, n1, lc = q_ref.shape
    y = _dot(m1i_ref[...].astype(BF16), q_ref[0].reshape(2 * n1, lc))
    y_ref[...] = y.reshape(2, n1 // 2, lc).astype(BF16)


def _hy_out(q, seq):
    pairs, _, n1, cols = q.shape
    lc = min(cols, 4096)
    m1i = jnp.asarray(_fft_constants(seq)["m1_inv"])
    y = pl.pallas_call(
        _hy_out_kernel,
        grid=(pairs, cols // lc),
        in_specs=[pl.BlockSpec((1, 2, n1, lc), lambda j, c: (j, 0, 0, c)),
                  pl.BlockSpec((n1, 2 * n1), lambda j, c: (0, 0))],
        out_specs=pl.BlockSpec((2, n1 // 2, lc), lambda j, c: (j, 0, c)),
        out_shape=jax.ShapeDtypeStruct((2 * pairs, n1 // 2, cols), BF16),
        compiler_params=_cparams(("arbitrary", "arbitrary")),
        name="hyena_out",
    )(q, m1i)
    return y.reshape(2 * pairs, seq, HY_WIDTH)


def _hy_filter_kernel(z_ref, t_ref, w1_ref, b1_ref, w2_ref, b2_ref, w3_ref, b3_ref, w4_ref, fr_ref, dl_ref,
                      taps_ref, inv_ref):
    fr = fr_ref[...]
    hdn = jnp.sin(fr * (_dot_hp(z_ref[...], w1_ref[...]) + b1_ref[...]))
    hdn = jnp.sin(fr * (_dot_hp(hdn, w2_ref[...]) + b2_ref[...]))
    hdn = jnp.sin(fr * (_dot_hp(hdn, w3_ref[...]) + b3_ref[...]))
    h = _dot_hp(hdn, w4_ref[...]) * jnp.exp(-t_ref[...] * dl_ref[...])
    h_fwd, h_bwd = h[:, :HY_WIDTH], h[:, HY_WIDTH:]
    row = lax.broadcasted_iota(jnp.int32, h_bwd.shape, 0)
    h_bwd = jnp.where(row == 0, 0.0, h_bwd)
    norm = jnp.sum(jnp.abs(h_fwd), 0, keepdims=True) + jnp.sum(jnp.abs(h_bwd), 0, keepdims=True)
    inv_ref[...] = 1.0 / norm
    taps_ref[0] = h_fwd
    taps_ref[1] = h_bwd


def _hy_filter(seq, w1, b1, w2, b2, w3, b3, w4, freq):
    t = jnp.linspace(0.0, 1.0, seq, dtype=F32)[:, None]
    w = 2.0 * math.pi * jnp.arange(seq, dtype=F32)[:, None] / seq
    f = jnp.linspace(1e-4, HY_BANDS - 1, HY_BANDS, dtype=F32)[None, :]
    z = jnp.concatenate([t, jnp.cos(f * w), -jnp.sin(f * w)], -1)
    z = jnp.pad(z, ((0, 0), (0, HY_ORDER - HY_EMB)))
    w1p = jnp.pad(w1, ((0, HY_ORDER - HY_EMB), (0, 0)))
    deltas = jnp.abs(jnp.linspace(math.log(HY_TARGET) / HY_SLOW, math.log(HY_TARGET) / HY_FAST, HY_WIDTH, dtype=F32))
    row = lambda v: v.reshape(1, -1)
    return pl.pallas_call(
        _hy_filter_kernel,
        out_shape=[jax.ShapeDtypeStruct((2, seq, HY_WIDTH), F32), jax.ShapeDtypeStruct((1, HY_WIDTH), F32)],
        compiler_params=pltpu.CompilerParams(vmem_limit_bytes=VMEM_LIMIT),
        name="hyena_filter",
    )(z, t, w1p, row(b1), w2, row(b2), w3, row(b3), w4, row(freq), row(jnp.tile(deltas, 2)))


def _hyena(pb, hy, seq):
    conv_w, conv_b, w1, b1, w2, b2, w3, b3, w4, freq, d_skip = hy
    taps, inv_norm = _hy_filter(seq, w1, b1, w2, b2, w3, b3, w4, freq)
    h_spec = _hy_filter_spec(_hy_stage1(taps, seq, False), inv_norm, seq)
    u, x0c = _hy_prep(pb, conv_w, conv_b)
    y = _hy_out(_hy_mid(_hy_stage1(u, seq, True), h_spec, seq), seq)
    flat = lambda a: a.reshape(-1, HY_WIDTH)
    return flat(y), flat(u), flat(x0c), d_skip.reshape(1, HY_WIDTH)


def _take_cols(w, idx):
    idx = np.asarray(idx)
    return jnp.where(jnp.asarray(idx >= 0)[None, :], w[:, np.maximum(idx, 0)], 0.0)


def _mix_weights(w_in, w_out, a_qn, a_kn, q_g, kv_g, w_uq, w_ukv):
    col = np.arange
    q_cols = np.concatenate([col(h * HEAD_DIM, (h + 1) * HEAD_DIM) for h in A_HEAD_ORDER])
    pad = lambda n: -np.ones(n, np.int64)
    in_idx = np.concatenate([q_cols, col(A_WIDTH, OFF_KR), pad(HEAD_DIM), col(OFF_KR, OFF_KR + C_ROPE),
                             pad(LANES - HEAD_DIM - C_ROPE)])
    dq = C_NOPE + C_ROPE
    uq_idx = np.concatenate([np.concatenate([col(h * dq, (h + 1) * dq), pad(LANES - dq)]) for h in range(C_HEADS)])
    dkv = C_NOPE + C_V
    uk_idx = np.concatenate([np.concatenate([col(h * dkv, h * dkv + C_NOPE), pad(LANES - C_NOPE)])
                             for h in range(C_HEADS)])
    uv_idx = np.concatenate([np.concatenate([col(h * dkv + C_NOPE, (h + 1) * dkv), pad(VT_ROWS - C_V)])
                             for h in range(C_HEADS)])
    half = np.arange(LANES) // HEAD_DIM
    e = (half[:, None] == half[None, :]).astype(np.float32)
    tile2 = lambda g: jnp.tile(g, LANES // HEAD_DIM).reshape(1, LANES)
    return dict(
        w_in=_take_cols(w_in, in_idx).astype(BF16),
        e=jnp.asarray(e, BF16),
        gq=tile2(a_qn), gk=tile2(a_kn),
        cq_g=q_g.reshape(1, -1), ckv_g=kv_g.reshape(1, -1),
        w_uq=_take_cols(w_uq, uq_idx).astype(BF16),
        w_uk=_take_cols(w_ukv, uk_idx).astype(BF16),
        w_uv=_take_cols(w_ukv, uv_idx).T.astype(BF16),
        wa=w_out[q_cols].astype(BF16),
        wb=w_out[A_WIDTH:A_WIDTH + HY_WIDTH].astype(BF16),
        wc=w_out[A_WIDTH + HY_WIDTH:].astype(BF16),
    )


def _rope_tables(seq):
    rows = seq // GRID_W
    row = jnp.repeat(jnp.arange(rows, dtype=F32), GRID_W)
    colv = jnp.tile(jnp.arange(GRID_W, dtype=F32), rows)

    def cos_sin(rot_dim):
        n_freq = rot_dim // 4
        inv = ROPE_THETA ** (-jnp.arange(n_freq, dtype=F32) / n_freq)
        ang = jnp.concatenate([row[:, None] * inv, colv[:, None] * inv], -1)
        return jnp.cos(ang), jnp.sin(ang)

    cos, sin = cos_sin(HEAD_DIM)
    cos_a = jnp.tile(cos, (1, 2 * LANES // HEAD_DIM))
    sin_a = jnp.tile(jnp.concatenate([-sin, sin], -1), (1, LANES // HEAD_DIM))
    cos, sin = cos_sin(C_ROPE)
    ones = jnp.ones((seq, C_NOPE), F32)
    tail = LANES - C_NOPE - C_ROPE
    cos_c = jnp.concatenate([ones, cos, cos, jnp.ones((seq, tail), F32)], -1)
    sin_c = jnp.concatenate([0.0 * ones, -sin, sin, jnp.zeros((seq, tail), F32)], -1)
    return cos_a, sin_a, cos_c, sin_c


def kernel(x, c, ctx, c_ctx, ada_w, ada_b, ffn1_w_gu, ffn1_w_down, ffn2_w_gu, ffn2_w_down, ln_g, ln_b, w_in, w_out,
           a_q_norm, a_k_norm, hy_conv_w, hy_conv_b, hy_f_w1, hy_f_b1, hy_f_w2, hy_f_b2, hy_f_w3, hy_f_b3, hy_f_w4,
           hy_f_freq, hy_bias, mla_q_norm, mla_kv_norm, mla_w_uq, mla_w_ukv):
    b, seq, d = x.shape
    n_ctx = ctx.shape[1]
    depth = ada_w.shape[0]
    assert b % 2 == 0 and seq % GRID_W == 0

    tm = min(512, seq)
    tm_ctx = min(512, b * n_ctx)
    tiles_per_seq = seq // tm
    lat_batch = lambda i: i // tiles_per_seq
    tm_mix = min(1024, seq)
    mix_tiles_per_seq = seq // tm_mix
    mix_batch = lambda i: i // mix_tiles_per_seq
    ctx_batch = lambda i: b

    mod_rows = ((b + 1 + 7) // 8) * 8
    c_all = jnp.concatenate([c, c_ctx[None], jnp.zeros((mod_rows - b - 1, d), F32)], 0)
    mods = _ada_mod(c_all, ada_w, ada_b).reshape(depth, mod_rows, N_MOD // 3, 3, d)

    rope_tabs = _rope_tables(seq)
    x2 = x.reshape(b * seq, d)
    ctx2 = ctx.reshape(b * n_ctx, d)

    for l in range(depth):
        need_ctx = l < depth - 1
        m4 = mods[l]
        ffn1 = (ffn1_w_gu[l].astype(BF16), ffn1_w_down[l].astype(BF16))
        ffn2 = (ffn2_w_gu[l].astype(BF16), ffn2_w_down[l].astype(BF16))
        wts = _mix_weights(w_in[l], w_out[l], a_q_norm[l], a_k_norm[l], mla_q_norm[l], mla_kv_norm[l],
                           mla_w_uq[l], mla_w_ukv[l])
        hy = (hy_conv_w[l], hy_conv_b[l], hy_f_w1[l], hy_f_b1[l], hy_f_w2[l], hy_f_b2[l], hy_f_w3[l], hy_f_b3[l],
              hy_f_w4[l], hy_f_freq[l], hy_bias[l])

        x2 = _ffn_sublayer(x2, m4, 0, lat_batch, *ffn1, ln_g[l, 0], ln_b[l, 0], tm)
        ctx2 = _ffn_sublayer(ctx2, m4, 0, ctx_batch, *ffn1, ln_g[l, 0], ln_b[l, 0], tm_ctx)

        qa, ka, va, pb, qc, kc, vc = _mix_in(x2, m4, mix_batch, wts, rope_tabs, mix_tiles_per_seq, tm_mix)
        qa_c, ka_c, va_c, pb_c, qc_c, kc_c, vc_c = _mix_in(ctx2, m4, ctx_batch, wts, None, 1, tm_ctx)
        r3 = lambda a, n: a.reshape(b, n, a.shape[-1])
        ka_c, kc_c = r3(ka_c, n_ctx), r3(kc_c, n_ctx)

        oa = _gqa_attention(r3(qa, seq), [(ka_c, va_c), (r3(ka, seq), va)], *ATTN_TILES_A)
        oc = _mla_attention(r3(qc, seq), [(kc_c, vc_c), (r3(kc, seq), vc)], *ATTN_TILES_C)
        hy_parts = _hyena(r3(pb, seq), hy, seq)
        flat = lambda a: a.reshape(-1, a.shape[-1])
        out_w = (wts["wa"], wts["wb"], wts["wc"], ln_g[l, 1], ln_b[l, 1])
        if need_ctx:
            oa_c = _gqa_attention(r3(qa_c, n_ctx), [(ka_c, va_c)], *ATTN_TILES_A)
            oc_c = _mla_attention(r3(qc_c, n_ctx), [(kc_c, vc_c)], *ATTN_TILES_C)
            hy_parts_c = _hyena(r3(pb_c, n_ctx), hy, n_ctx)
            ctx2 = _ffn_sublayer(ctx2, m4, 2, ctx_batch, *ffn2, ln_g[l, 2], ln_b[l, 2], tm_ctx,
                                 mix=(flat(oa_c), hy_parts_c, flat(oc_c), *out_w))
        x2 = _ffn_sublayer(x2, m4, 2, lat_batch, *ffn2, ln_g[l, 2], ln_b[l, 2], tm,
                           mix=(flat(oa), hy_parts, flat(oc), *out_w))
    return x2.reshape(b, seq, d)
```

```python
import functools
import math

import numpy as np
import jax
import jax.numpy as jnp
from jax import lax
from jax.experimental import pallas as pl
from jax.experimental.pallas import tpu as pltpu

F32 = jnp.float32
BF16 = jnp.bfloat16

D_MODEL = 1024
GRID_W = 64
N_MOD = 9
D_FF = 2816
DEPTH = 2
ALPHA = (2 * DEPTH) ** 0.25
EPS = 1e-6
A_HEADS = 6
A_KV_HEADS = 2
HEAD_DIM = 64
ROPE_THETA = 10000.0
HY_WIDTH = 256
HY_EMB = 33
HY_BANDS = (HY_EMB - 1) // 2
HY_ORDER = 64
HY_TARGET = 1e-2
HY_FAST = 0.3
HY_SLOW = 1.5
C_HEADS = 6
C_Q_LORA = 256
C_KV_LORA = 128
C_NOPE = 64
C_ROPE = 32
C_V = 64
A_WIDTH = A_HEADS * HEAD_DIM
C_WIDTH = C_HEADS * C_V
A_COLS = (A_HEADS + 2 * A_KV_HEADS) * HEAD_DIM
B_COLS = 3 * HY_WIDTH
C_COLS = C_Q_LORA + C_KV_LORA + C_ROPE

LANES = 128
MXU_DIM = 256
VMEM_LIMIT = 56 * 1024 * 1024

ROW_TILE = 512
MIX_ROW_TILE = 1024
HY_COL_TILE = 8192
HY_PLANES = 16

P_COLS = A_COLS + B_COLS + C_Q_LORA + C_KV_LORA + LANES
OFF_K = A_WIDTH
OFF_V = A_WIDTH + A_KV_HEADS * HEAD_DIM
OFF_B = A_COLS
OFF_CQ = A_COLS + B_COLS
OFF_CKV = OFF_CQ + C_Q_LORA
OFF_KR = OFF_CKV + C_KV_LORA
A_HEAD_ORDER = (0, 3, 1, 4, 2, 5)
CQ_PAD = C_HEADS * LANES


def _cparams(sem):
    return pltpu.CompilerParams(dimension_semantics=sem, vmem_limit_bytes=VMEM_LIMIT)


def _layer_norm(x):
    mu = jnp.mean(x, -1, keepdims=True)
    xc = x - mu
    var = jnp.mean(xc * xc, -1, keepdims=True)
    return xc * lax.rsqrt(var + EPS)


def _silu(x):
    return x * jax.nn.sigmoid(x)


def _dot(a, b):
    return jnp.dot(a, b, preferred_element_type=F32)


def _dot_nt(a, b):
    return lax.dot_general(a, b, (((1,), (1,)), ((), ())), preferred_element_type=F32)


def _dot_hp(a, b):
    return jnp.dot(a, b, preferred_element_type=F32, precision=lax.Precision.HIGHEST)


def _ada_kernel(c_ref, w_ref, b_ref, o_ref):
    a = _silu(c_ref[...]).astype(BF16)
    o_ref[0] = _dot(a, w_ref[0].astype(BF16)) + b_ref[0]


def _ada_mod(c_all, ada_w, ada_b):
    depth, d, n = ada_w.shape
    rows = c_all.shape[0]
    tn = n // 8
    return pl.pallas_call(
        _ada_kernel,
        grid=(depth, n // tn),
        in_specs=[
            pl.BlockSpec((rows, d), lambda l, j: (0, 0)),
            pl.BlockSpec((1, d, tn), lambda l, j: (l, 0, j)),
            pl.BlockSpec((1, 1, tn), lambda l, j: (l, 0, j)),
        ],
        out_specs=pl.BlockSpec((1, rows, tn), lambda l, j: (l, 0, j)),
        out_shape=jax.ShapeDtypeStruct((depth, rows, n), F32),
        compiler_params=_cparams(("arbitrary", "arbitrary")),
        name="ada_mod",
    )(c_all, ada_w, ada_b.reshape(depth, 1, n))


def _ffn_kernel(*refs, bounds, fused_mix):
    if fused_mix:
        (xin_ref, modm_ref, oa_ref, y_ref, u_ref, x0_ref, d_ref, oc_ref, wa_ref, wb_ref, wc_ref, g1_ref, b1_ref,
         mod_ref, wg_ref, wu_ref, wd_ref, g_ref, b_ref, o_ref, x_ref) = refs
    else:
        x_ref, mod_ref, wg_ref, wu_ref, wd_ref, g_ref, b_ref, o_ref = refs
    mod = mod_ref[0, 0]
    half = x_ref.shape[0] // 2
    halves = (slice(0, half), slice(half, 2 * half))
    n_chunks = len(bounds) - 1

    def mix_rows(r):
        ob = (y_ref[r, :].astype(F32) + u_ref[r, :].astype(F32) * d_ref[...]) * x0_ref[r, :].astype(F32)
        mix = (_dot(oa_ref[r, :], wa_ref[...]) + _dot(ob.astype(BF16), wb_ref[...])
               + _dot(oc_ref[r, :], wc_ref[...]))
        y = ALPHA * xin_ref[r, :] + modm_ref[0, 0][2:3] * mix
        x_ref[r, :] = _layer_norm(y) * g1_ref[...] + b1_ref[...]

    hs = []
    for r in halves:
        if fused_mix:
            mix_rows(r)
        hs.append((_layer_norm(x_ref[r, :]) * (1.0 + mod[1:2]) + mod[0:1]).astype(BF16))
    acc = None
    for k, (lo, hi) in enumerate(zip(bounds[:-1], bounds[1:])):
        if k == 0:
            a = jnp.concatenate([(_silu(_dot(hr, wg_ref[:, lo:hi])) * _dot(hr, wu_ref[:, lo:hi])).astype(BF16)
                                 for hr in hs], axis=0)
            h = jnp.concatenate(hs, axis=0)
        else:
            a = (_silu(_dot(h, wg_ref[:, lo:hi])) * _dot(h, wu_ref[:, lo:hi])).astype(BF16)
        if k == n_chunks - 1:
            for r in halves:
                part = _dot(a[r, :], wd_ref[lo:hi, :])
                y = ALPHA * x_ref[r, :] + (0.5 * mod[2:3]) * (part if acc is None else acc[r, :] + part)
                o_ref[r, :] = _layer_norm(y) * g_ref[...] + b_ref[...]
        else:
            part = _dot(a, wd_ref[lo:hi, :])
            acc = part if acc is None else acc + part


def _ffn_sublayer(x2, mods4, mod_group, batch_of_tile, wg, wu, wd, g, b, tm, mix=None):
    rows, d = x2.shape
    f = wg.shape[1]
    const = lambda i: (0, 0)
    row_spec = lambda w: pl.BlockSpec((tm, w), lambda i: (i, 0))
    mod_spec = lambda group: pl.BlockSpec((1, 1, 3, d), lambda i: (batch_of_tile(i), group, 0, 0))
    vec_spec = pl.BlockSpec((1, d), const)
    n_tiles = -(-f // MXU_DIM)
    bounds = (0, min(f, ((n_tiles + 1) // 2) * MXU_DIM), f)
    in_specs, args, scratch = [row_spec(d)], [x2], []
    if mix is not None:
        oa, (y, u, x0c, d_skip), oc, wa, wb, wc, g1, b1 = mix
        in_specs += [mod_spec(1), row_spec(A_WIDTH), row_spec(HY_WIDTH), row_spec(HY_WIDTH), row_spec(HY_WIDTH),
                     pl.BlockSpec((1, HY_WIDTH), const), row_spec(C_WIDTH),
                     pl.BlockSpec((A_WIDTH, d), const), pl.BlockSpec((HY_WIDTH, d), const),
                     pl.BlockSpec((C_WIDTH, d), const), vec_spec, vec_spec]
        args += [mods4, oa, y, u, x0c, d_skip, oc, wa, wb, wc, g1.reshape(1, d), b1.reshape(1, d)]
        scratch = [pltpu.VMEM((tm, d), F32)]
    in_specs += [mod_spec(mod_group),
                 pl.BlockSpec((d, f), const, pipeline_mode=pl.Buffered(1)),
                 pl.BlockSpec((d, f), const, pipeline_mode=pl.Buffered(1)),
                 pl.BlockSpec((f, d), const, pipeline_mode=pl.Buffered(1)),
                 vec_spec, vec_spec]
    args += [mods4, wg, wu, wd, g.reshape(1, d), b.reshape(1, d)]
    return pl.pallas_call(
        functools.partial(_ffn_kernel, bounds=bounds, fused_mix=mix is not None),
        grid=(rows // tm,),
        in_specs=in_specs,
        out_specs=row_spec(d),
        out_shape=jax.ShapeDtypeStruct((rows, d), F32),
        scratch_shapes=scratch,
        compiler_params=_cparams(("arbitrary",)),
        name="mix_out_ffn" if mix is not None else "ffn_sublayer",
    )(*args)


def _rope(x, cos, sin_signed, half):
    lane = lax.broadcasted_iota(jnp.int32, x.shape, 1)
    first = (lane % (2 * half)) < half
    partner = jnp.where(first, pltpu.roll(x, LANES - half, 1), pltpu.roll(x, half, 1))
    return x * cos + partner * sin_signed


def _half_tile_mean_sq(x, e_ref):
    return _dot((x * x).astype(BF16), e_ref[...]) * (1.0 / HEAD_DIM)


def _mix_in_kernel(x_ref, mod_ref, w_ref, e_ref, gq_ref, gk_ref, cq_g_ref, ckv_g_ref, wuq_ref, wuk_ref, wuv_ref,
                   *rest, use_rope):
    if use_rope:
        cos_a_ref, sin_a_ref, cos_c_ref, sin_c_ref = rest[:4]
        rest = rest[4:]
    qa_ref, ka_ref, va_ref, pb_ref, qc_ref, kc_ref, vc_ref = rest

    mod = mod_ref[0, 0]
    half = x_ref.shape[0] // 2
    scale_c = LOG2E * (C_NOPE + C_ROPE) ** -0.5
    for r in (slice(0, half), slice(half, 2 * half)):
        h = (_layer_norm(x_ref[r, :]) * (1.0 + mod[1:2]) + mod[0:1]).astype(BF16)
        p = _dot(h, w_ref[...])

        def rope_a(t):
            return _rope(t, cos_a_ref[r, :], sin_a_ref[r, :], HEAD_DIM // 2) if use_rope else t

        def rope_c(t):
            return _rope(t, cos_c_ref[r, :], sin_c_ref[r, :], C_ROPE // 2) if use_rope else t

        for c in range(A_WIDTH // LANES):
            t = p[:, c * LANES:(c + 1) * LANES]
            t = t * lax.rsqrt(_half_tile_mean_sq(t, e_ref) + EPS) * gq_ref[...]
            qa_ref[r, c * LANES:(c + 1) * LANES] = (rope_a(t) * (LOG2E * HEAD_DIM ** -0.5)).astype(BF16)
        t = p[:, OFF_K:OFF_K + LANES]
        t = t * lax.rsqrt(_half_tile_mean_sq(t, e_ref) + EPS) * gk_ref[...]
        ka_ref[r, :] = rope_a(t).astype(BF16)
        v = p[:, OFF_V:OFF_V + LANES]
        low = lax.broadcasted_iota(jnp.int32, v.shape, 1) < HEAD_DIM
        va_ref[r, 0:LANES] = jnp.where(low, v, 1.0).astype(BF16)
        va_ref[r, LANES:2 * LANES] = jnp.where(low, 1.0, v).astype(BF16)

        pb_ref[r, :] = p[:, OFF_B:OFF_B + B_COLS].astype(BF16)

        c_q = p[:, OFF_CQ:OFF_CQ + C_Q_LORA]
        c_q = c_q * lax.rsqrt(jnp.mean(c_q * c_q, -1, keepdims=True) + EPS) * cq_g_ref[...]
        qq = _dot(c_q.astype(BF16), wuq_ref[...])
        c_kv = p[:, OFF_CKV:OFF_CKV + C_KV_LORA]
        c_kv = (c_kv * lax.rsqrt(jnp.mean(c_kv * c_kv, -1, keepdims=True) + EPS) * ckv_g_ref[...]).astype(BF16)
        kn = _dot(c_kv, wuk_ref[...])
        vv = _dot(c_kv, wuv_ref[...])
        k_r = rope_c(p[:, OFF_KR:OFF_KR + LANES])
        for hd in range(C_HEADS):
            sl = slice(hd * LANES, (hd + 1) * LANES)
            qc_ref[r, sl] = (rope_c(qq[:, sl]) * scale_c).astype(BF16)
            kc_ref[r, sl] = (kn[:, sl] + k_r).astype(BF16)
            ones_low = (hd % 2) == 1
            vc_ref[r, sl] = jnp.where(low != ones_low, vv[:, sl], 1.0).astype(BF16)


def _mix_in(x2, mods4, batch_of_tile, wts, rope_tabs, tiles_per_seq, tm):
    rows, d = x2.shape
    const = lambda i: (0, 0)
    use_rope = rope_tabs is not None
    in_specs = [
        pl.BlockSpec((tm, d), lambda i: (i, 0)),
        pl.BlockSpec((1, 1, 3, d), lambda i: (batch_of_tile(i), 1, 0, 0)),
        pl.BlockSpec((d, P_COLS), const, pipeline_mode=pl.Buffered(1)),
        pl.BlockSpec((LANES, LANES), const),
        pl.BlockSpec((1, LANES), const),
        pl.BlockSpec((1, LANES), const),
        pl.BlockSpec((1, C_Q_LORA), const),
        pl.BlockSpec((1, C_KV_LORA), const),
        pl.BlockSpec((C_Q_LORA, CQ_PAD), const),
        pl.BlockSpec((C_KV_LORA, CQ_PAD), const),
        pl.BlockSpec((C_KV_LORA, CQ_PAD), const),
    ]
    args = [x2, mods4, wts["w_in"], wts["e"], wts["gq"], wts["gk"], wts["cq_g"], wts["ckv_g"],
            wts["w_uq"], wts["w_uk"], wts["w_uv"]]
    if use_rope:
        in_specs += [pl.BlockSpec((tm, LANES), lambda i: (i % tiles_per_seq, 0))] * 4
        args += list(rope_tabs)
    row_spec = lambda w: pl.BlockSpec((tm, w), lambda i: (i, 0))
    widths = (A_WIDTH, LANES, 2 * LANES, B_COLS, CQ_PAD, CQ_PAD, CQ_PAD)
    return pl.pallas_call(
        functools.partial(_mix_in_kernel, use_rope=use_rope),
        grid=(rows // tm,),
        in_specs=in_specs,
        out_specs=[row_spec(w) for w in widths],
        out_shape=[jax.ShapeDtypeStruct((rows, w), BF16) for w in widths],
        compiler_params=_cparams(("arbitrary",)),
        name="mix_in_rope" if use_rope else "mix_in",
    )(*args)


ATTN_TILES_A = (256, 512, 2)
ATTN_TILES_C = (512, 512)
LOG2E = math.log2(math.e)


def _key_chunks(kv_refs, tk):
    chunks = []
    for s in range(len(kv_refs) // 2):
        n_rows = kv_refs[2 * s].shape[1]
        size = min(tk, n_rows)
        chunks += [(s, off, size) for off in range(0, n_rows, size)]
    return chunks


def _flash(q_get, n_chain, chunks, k_get, v_get, scratch):
    s_refs = [scratch[2 * c:2 * c + 2] for c in range(n_chain)]
    m_refs = scratch[2 * n_chain:3 * n_chain]
    acc_refs = scratch[3 * n_chain:4 * n_chain]

    def scores(i):
        size = chunks[i][2]
        for c in range(n_chain):
            s_refs[c][i % 2][:, 0:size] = _dot_nt(q_get(c), k_get(c, chunks[i]))

    def softmax_pv(i):
        size = chunks[i][2]
        for c in range(n_chain):
            slabs = [s_refs[c][i % 2][:, j * LANES:(j + 1) * LANES] for j in range(size // LANES)]
            m_new = functools.reduce(jnp.maximum, slabs)
            m_new = jnp.broadcast_to(jnp.max(m_new, axis=-1, keepdims=True), m_new.shape)
            if i > 0:
                m_old = m_refs[c][...]
                m_new = jnp.maximum(m_old, m_new)
            p = jnp.concatenate([jnp.exp2(sl - m_new).astype(BF16) for sl in slabs], axis=1)
            pv = _dot(p, v_get(c, chunks[i]))
            acc_refs[c][...] = pv if i == 0 else jnp.exp2(m_old - m_new) * acc_refs[c][...] + pv
            m_refs[c][...] = m_new

    scores(0)
    for i in range(len(chunks)):
        if i + 1 < len(chunks):
            scores(i + 1)
        softmax_pv(i)
    return acc_refs


def _attn_scratch(n_chain, m_rows, tk):
    return ([pltpu.VMEM((m_rows, tk), F32)] * (2 * n_chain) + [pltpu.VMEM((m_rows, LANES), F32)] * (2 * n_chain))


def _merge_halves(acc_low, acc_high):
    lane = lax.broadcasted_iota(jnp.int32, acc_low.shape, 1)
    low = lane < HEAD_DIM
    num = jnp.where(low, acc_low, acc_high)
    den = pltpu.roll(jnp.where(low, acc_high, acc_low), HEAD_DIM, 1)
    return num / den


def _gqa_kernel(q_ref, *refs, n_src, tk, n_sub):
    kv_refs, o_ref = refs[:2 * n_src], refs[2 * n_src]
    scratch = refs[2 * n_src + 1:]
    per_sub = len(scratch) // n_sub
    tq = q_ref.shape[1] // n_sub
    n_tiles = A_WIDTH // LANES
    low = lax.broadcasted_iota(jnp.int32, (tq, LANES), 1) < HEAD_DIM
    zero = jnp.zeros((tq, LANES), BF16)
    k_get = lambda g, ch: kv_refs[2 * ch[0]][0, ch[1]:ch[1] + ch[2], :]
    v_get = lambda g, ch: kv_refs[2 * ch[0] + 1][0, ch[1]:ch[1] + ch[2], g * LANES:(g + 1) * LANES]
    for t in range(n_sub):
        qs_ref, sub_scratch = scratch[t * per_sub], scratch[t * per_sub + 1:(t + 1) * per_sub]
        q_rows = slice(t * tq, (t + 1) * tq)
        for g in range(A_KV_HEADS):
            for c in range(n_tiles):
                qs_ref[g, c * tq:(c + 1) * tq, :] = jnp.where(
                    low == (g == 0), q_ref[0, q_rows, c * LANES:(c + 1) * LANES], zero)
        q_get = lambda g, qs_ref=qs_ref: qs_ref[g]
        acc = _flash(q_get, A_KV_HEADS, _key_chunks(kv_refs, tk), k_get, v_get, sub_scratch)
        for c in range(n_tiles):
            rows = slice(c * tq, (c + 1) * tq)
            o_ref[0, q_rows, c * LANES:(c + 1) * LANES] = _merge_halves(acc[0][rows, :], acc[1][rows, :]).astype(BF16)


def _gqa_attention(q, kv_list, tq, tk, n_sub):
    b, lq, _ = q.shape
    tk = min(tk, max(k.shape[1] for k, _ in kv_list))
    if lq < tq * n_sub:
        tq, n_sub = min(tq, lq), 1
    m_rows = (A_HEADS // A_KV_HEADS) * tq
    tq = tq * n_sub
    in_specs = [pl.BlockSpec((1, tq, A_WIDTH), lambda i, j: (i, j, 0))]
    args = [q]
    for k, v in kv_list:
        in_specs.append(pl.BlockSpec((1, k.shape[1], LANES), lambda i, j: (i, 0, 0)))
        in_specs.append(pl.BlockSpec((1, v.shape[1], 2 * LANES), lambda i, j: (i, 0, 0)))
        args += [k, v]
    return pl.pallas_call(
        functools.partial(_gqa_kernel, n_src=len(kv_list), tk=tk, n_sub=n_sub),
        grid=(b, lq // tq),
        in_specs=in_specs,
        out_specs=pl.BlockSpec((1, tq, A_WIDTH), lambda i, j: (i, j, 0)),
        out_shape=jax.ShapeDtypeStruct((b, lq, A_WIDTH), BF16),
        scratch_shapes=([pltpu.VMEM((A_KV_HEADS, m_rows, LANES), BF16)]
                        + _attn_scratch(A_KV_HEADS, m_rows, tk)) * n_sub,
        compiler_params=_cparams(("arbitrary", "arbitrary")),
        name="gqa_attention",
    )(*args)


def _mla_kernel(q_ref, *refs, n_src, tk):
    kv_refs, o_ref = refs[:2 * n_src], refs[2 * n_src]
    scratch = refs[2 * n_src + 1:]
    n_pairs = C_HEADS // 2
    per_pair = len(scratch) // n_pairs
    for hp in range(n_pairs):
        lanes = lambda e, hp=hp: slice((2 * hp + e) * LANES, (2 * hp + e + 1) * LANES)
        q_get = lambda e, lanes=lanes: q_ref[0, :, lanes(e)]
        k_get = lambda e, ch, lanes=lanes: kv_refs[2 * ch[0]][0, ch[1]:ch[1] + ch[2], lanes(e)]
        v_get = lambda e, ch, lanes=lanes: kv_refs[2 * ch[0] + 1][0, ch[1]:ch[1] + ch[2], lanes(e)]
        acc = _flash(q_get, 2, _key_chunks(kv_refs, tk), k_get, v_get, scratch[hp * per_pair:(hp + 1) * per_pair])
        o_ref[0, :, hp * LANES:(hp + 1) * LANES] = _merge_halves(acc[0][...], acc[1][...]).astype(BF16)


def _mla_attention(q, kv_list, tq, tk):
    b, lq, _ = q.shape
    tq, tk = min(tq, lq), min(tk, max(k.shape[1] for k, _ in kv_list))
    in_specs = [pl.BlockSpec((1, tq, CQ_PAD), lambda i, j: (i, j, 0))]
    args = [q]
    for k, v in kv_list:
        in_specs.append(pl.BlockSpec((1, k.shape[1], CQ_PAD), lambda i, j: (i, 0, 0)))
        in_specs.append(pl.BlockSpec((1, v.shape[1], CQ_PAD), lambda i, j: (i, 0, 0)))
        args += [k, v]
    return pl.pallas_call(
        functools.partial(_mla_kernel, n_src=len(kv_list), tk=tk),
        grid=(b, lq // tq),
        in_specs=in_specs,
        out_specs=pl.BlockSpec((1, tq, C_WIDTH), lambda i, j: (i, j, 0)),
        out_shape=jax.ShapeDtypeStruct((b, lq, C_WIDTH), BF16),
        scratch_shapes=_attn_scratch(2, tq, tk) * (C_HEADS // 2),
        compiler_params=_cparams(("arbitrary", "arbitrary")),
        name="mla_attention",
    )(*args)


def _fft_dims(seq):
    n = 2 * seq
    n2 = min(LANES, n // 16)
    return n // n2, n2


def _stack_real(m):
    return np.block([[m.real, -m.imag], [m.imag, m.real]]).astype(np.float32)


@functools.lru_cache(maxsize=None)
def _fft_constants(seq):
    n1, n2 = _fft_dims(seq)
    n = n1 * n2
    k1 = np.arange(n1)[:, None]
    f1 = np.exp(-2j * np.pi * k1 * np.arange(n1 // 2)[None, :] / n1)
    f2 = np.exp(-2j * np.pi * np.arange(n2)[:, None] * np.arange(n2)[None, :] / n2)
    tw = np.exp(-2j * np.pi * k1 * np.arange(n2)[None, :] / n)
    g1 = np.conj(f1).T / n
    return dict(
        m1=_stack_real(f1), m2=_stack_real(f2), m2_inv=_stack_real(np.conj(f2)), m1_inv=_stack_real(g1),
        tw_r=tw.real.astype(np.float32)[:, :, None], tw_i=tw.imag.astype(np.float32)[:, :, None])


def _hy_prep_kernel(v_ref, x1_ref, x0_ref, wv_ref, wx1_ref, wx0_ref, bv_ref, bx1_ref, bx0_ref, u_ref, x0c_ref):
    seq = v_ref.shape[1]
    row = lax.broadcasted_iota(jnp.int32, (seq, LANES), 0)
    first, last = row == 0, row == seq - 1

    def conv(p_ref, w_ref, b_ref):
        t = p_ref[0].astype(F32)
        prev = jnp.where(first, 0.0, pltpu.roll(t, 1, 0))
        nxt = jnp.where(last, 0.0, pltpu.roll(t, seq - 1, 0))
        return prev * w_ref[0:1] + t * w_ref[1:2] + nxt * w_ref[2:3] + b_ref[...]

    u_ref[0] = (conv(v_ref, wv_ref, bv_ref) * conv(x1_ref, wx1_ref, bx1_ref)).astype(BF16)
    x0c_ref[0] = conv(x0_ref, wx0_ref, bx0_ref).astype(BF16)


def _hy_prep(pb, conv_w, conv_b):
    b, seq, _ = pb.shape
    n_c = HY_WIDTH // LANES
    p_spec = lambda part: pl.BlockSpec((1, seq, LANES), lambda i, c: (i, 0, part * n_c + c))
    w_spec = lambda part: pl.BlockSpec((3, LANES), lambda i, c: (0, part * n_c + c))
    b_spec = lambda part: pl.BlockSpec((1, LANES), lambda i, c: (0, part * n_c + c))
    out_spec = pl.BlockSpec((1, seq, LANES), lambda i, c: (i, 0, c))
    cb = conv_b.reshape(1, B_COLS)
    return pl.pallas_call(
        _hy_prep_kernel,
        grid=(b, n_c),
        in_specs=[p_spec(0), p_spec(1), p_spec(2), w_spec(0), w_spec(1), w_spec(2), b_spec(0), b_spec(1), b_spec(2)],
        out_specs=[out_spec, out_spec],
        out_shape=[jax.ShapeDtypeStruct((b, seq, HY_WIDTH), BF16)] * 2,
        compiler_params=_cparams(("arbitrary", "arbitrary")),
        name="hyena_prep",
    )(pb, pb, pb, conv_w, conv_w, conv_w, cb, cb, cb)


def _hy_stage1_kernel(u_ref, m1_ref, a_ref):
    members, h, lc = u_ref.shape
    z = u_ref[...].astype(F32).reshape(members * h, lc).astype(BF16)
    a = _dot(m1_ref[...].astype(BF16), z)
    a_ref[0] = a.reshape(2, a.shape[0] // 2, lc).astype(BF16)


def _hy_stage1(u, seq, paired):
    b, _, ch = u.shape
    n1, n2 = _fft_dims(seq)
    cols = n2 * ch
    lc = min(cols, HY_COL_TILE)
    members = 2 if paired else 1
    m1 = _fft_constants(seq)["m1"]
    m1 = jnp.asarray(m1 if paired else m1[:, :n1 // 2])
    return pl.pallas_call(
        _hy_stage1_kernel,
        grid=(b // members, cols // lc),
        in_specs=[pl.BlockSpec((members, n1 // 2, lc), lambda j, c: (j, 0, c)),
                  pl.BlockSpec(m1.shape, lambda j, c: (0, 0))],
        out_specs=pl.BlockSpec((1, 2, n1, lc), lambda j, c: (j, 0, 0, c)),
        out_shape=jax.ShapeDtypeStruct((b // members, 2, n1, cols), BF16),
        compiler_params=_cparams(("arbitrary", "arbitrary")),
        name="hyena_stage1",
    )(u.reshape(b, n1 // 2, cols), m1)


def _twiddle(ar, ai, tr, ti):
    return ar * tr - ai * ti, ar * ti + ai * tr


def _hy_filter_spec_kernel(a_ref, twr_ref, twi_ref, m2_ref, inv_ref, h_ref):
    n2 = a_ref.shape[3]
    m2 = m2_ref[...].astype(BF16)
    for p in range(a_ref.shape[2]):
        tr, ti = twr_ref[p], twi_ref[p]

        def fwd(seq_idx):
            br, bi = _twiddle(a_ref[seq_idx, 0, p].astype(F32), a_ref[seq_idx, 1, p].astype(F32), tr, ti)
            return _dot(m2, jnp.concatenate([br, bi], axis=0).astype(BF16))

        zf, zb = fwd(0), fwd(1)
        h_ref[p, 0:n2] = (zf[0:n2] + zb[0:n2]) * inv_ref[...]
        h_ref[p, n2:2 * n2] = (zf[n2:2 * n2] - zb[n2:2 * n2]) * inv_ref[...]


def _hy_filter_spec(a, inv_norm, seq):
    n1, n2 = _fft_dims(seq)
    cst = _fft_constants(seq)
    kp = min(HY_PLANES, n1)
    a5 = a.reshape(2, 2, n1, n2, HY_WIDTH)
    return pl.pallas_call(
        _hy_filter_spec_kernel,
        grid=(n1 // kp,),
        in_specs=[pl.BlockSpec((2, 2, kp, n2, HY_WIDTH), lambda k: (0, 0, k, 0, 0)),
                  pl.BlockSpec((kp, n2, 1), lambda k: (k, 0, 0)),
                  pl.BlockSpec((kp, n2, 1), lambda k: (k, 0, 0)),
                  pl.BlockSpec((2 * n2, 2 * n2), lambda k: (0, 0)),
                  pl.BlockSpec((1, HY_WIDTH), lambda k: (0, 0))],
        out_specs=pl.BlockSpec((kp, 2 * n2, HY_WIDTH), lambda k: (k, 0, 0)),
        out_shape=jax.ShapeDtypeStruct((n1, 2 * n2, HY_WIDTH), F32),
        compiler_params=_cparams(("arbitrary",)),
        name="hyena_filter_spectrum",
    )(a5, jnp.asarray(cst["tw_r"]), jnp.asarray(cst["tw_i"]), jnp.asarray(cst["m2"]), inv_norm)


def _hy_mid_kernel(a_ref, twr_ref, twi_ref, m2_ref, m2i_ref, h_ref, q_ref):
    n2 = a_ref.shape[3]
    m2, m2i = m2_ref[...].astype(BF16), m2i_ref[...].astype(BF16)
    for p in range(a_ref.shape[2]):
        tr, ti = twr_ref[p], twi_ref[p]
        br, bi = _twiddle(a_ref[0, 0, p].astype(F32), a_ref[0, 1, p].astype(F32), tr, ti)
        z = _dot(m2, jnp.concatenate([br, bi], axis=0).astype(BF16))
        zr, zi = z[0:n2], z[n2:2 * n2]
        hr, hi = h_ref[p, 0:n2], h_ref[p, n2:2 * n2]
        yr, yi = zr * hr - zi * hi, zr * hi + zi * hr
        y = _dot(m2i, jnp.concatenate([yr, yi], axis=0).astype(BF16))
        qr, qi = _twiddle(y[0:n2], y[n2:2 * n2], tr, -ti)
        q_ref[0, 0, p] = qr.astype(BF16)
        q_ref[0, 1, p] = qi.astype(BF16)


def _hy_mid(a, h_spec, seq):
    pairs = a.shape[0]
    n1, n2 = _fft_dims(seq)
    cst = _fft_constants(seq)
    kp = min(HY_PLANES, n1)
    a5 = a.reshape(pairs, 2, n1, n2, HY_WIDTH)
    blk = pl.BlockSpec((1, 2, kp, n2, HY_WIDTH), lambda k, j: (j, 0, k, 0, 0))
    q = pl.pallas_call(
        _hy_mid_kernel,
        grid=(n1 // kp, pairs),
        in_specs=[blk,
                  pl.BlockSpec((kp, n2, 1), lambda k, j: (k, 0, 0)),
                  pl.BlockSpec((kp, n2, 1), lambda k, j: (k, 0, 0)),
                  pl.BlockSpec((2 * n2, 2 * n2), lambda k, j: (0, 0)),
                  pl.BlockSpec((2 * n2, 2 * n2), lambda k, j: (0, 0)),
                  pl.BlockSpec((kp, 2 * n2, HY_WIDTH), lambda k, j: (k, 0, 0))],
        out_specs=blk,
        out_shape=jax.ShapeDtypeStruct((pairs, 2, n1, n2, HY_WIDTH), BF16),
        compiler_params=_cparams(("arbitrary", "arbitrary")),
        name="hyena_mid",
    )(a5, jnp.asarray(cst["tw_r"]), jnp.asarray(cst["tw_i"]), jnp.asarray(cst["m2"]), jnp.asarray(cst["m2_inv"]),
      h_spec)
    return q.reshape(pairs, 2, n1, n2 * HY_WIDTH)


def _hy_out_kernel(q_ref, m1i_ref, y_ref):
    _, _, n1, lc = q_ref.shape
    y = _dot(m1i_ref[...].astype(BF16), q_ref[0].reshape(2 * n1, lc))
    y_ref[...] = y.reshape(2, n1 // 2, lc).astype(BF16)


def _hy_out(q, seq):
    pairs, _, n1, cols = q.shape
    lc = min(cols, HY_COL_TILE)
    m1i = jnp.asarray(_fft_constants(seq)["m1_inv"])
    y = pl.pallas_call(
        _hy_out_kernel,
        grid=(pairs, cols // lc),
        in_specs=[pl.BlockSpec((1, 2, n1, lc), lambda j, c: (j, 0, 0, c)),
                  pl.BlockSpec((n1, 2 * n1), lambda j, c: (0, 0))],
        out_specs=pl.BlockSpec((2, n1 // 2, lc), lambda j, c: (j, 0, c)),
        out_shape=jax.ShapeDtypeStruct((2 * pairs, n1 // 2, cols), BF16),
        compiler_params=_cparams(("arbitrary", "arbitrary")),
        name="hyena_out",
    )(q, m1i)
    return y.reshape(2 * pairs, seq, HY_WIDTH)


def _hy_filter_kernel(z_ref, t_ref, w1_ref, b1_ref, w2_ref, b2_ref, w3_ref, b3_ref, w4_ref, fr_ref, dl_ref,
                      taps_ref, inv_ref):
    fr = fr_ref[...]
    hdn = jnp.sin(fr * (_dot_hp(z_ref[...], w1_ref[...]) + b1_ref[...]))
    hdn = jnp.sin(fr * (_dot_hp(hdn, w2_ref[...]) + b2_ref[...]))
    hdn = jnp.sin(fr * (_dot_hp(hdn, w3_ref[...]) + b3_ref[...]))
    h = _dot_hp(hdn, w4_ref[...]) * jnp.exp(-t_ref[...] * dl_ref[...])
    h_fwd, h_bwd = h[:, :HY_WIDTH], h[:, HY_WIDTH:]
    row = lax.broadcasted_iota(jnp.int32, h_bwd.shape, 0)
    h_bwd = jnp.where(row == 0, 0.0, h_bwd)
    norm = jnp.sum(jnp.abs(h_fwd), 0, keepdims=True) + jnp.sum(jnp.abs(h_bwd), 0, keepdims=True)
    inv_ref[...] = 1.0 / norm
    taps_ref[0] = h_fwd
    taps_ref[1] = h_bwd


def _hy_filter(seq, w1, b1, w2, b2, w3, b3, w4, freq):
    t = jnp.linspace(0.0, 1.0, seq, dtype=F32)[:, None]
    w = 2.0 * math.pi * jnp.arange(seq, dtype=F32)[:, None] / seq
    f = jnp.linspace(1e-4, HY_BANDS - 1, HY_BANDS, dtype=F32)[None, :]
    z = jnp.concatenate([t, jnp.cos(f * w), -jnp.sin(f * w)], -1)
    z = jnp.pad(z, ((0, 0), (0, HY_ORDER - HY_EMB)))
    w1p = jnp.pad(w1, ((0, HY_ORDER - HY_EMB), (0, 0)))
    deltas = jnp.abs(jnp.linspace(math.log(HY_TARGET) / HY_SLOW, math.log(HY_TARGET) / HY_FAST, HY_WIDTH, dtype=F32))
    row = lambda v: v.reshape(1, -1)
    return pl.pallas_call(
        _hy_filter_kernel,
        out_shape=[jax.ShapeDtypeStruct((2, seq, HY_WIDTH), F32), jax.ShapeDtypeStruct((1, HY_WIDTH), F32)],
        compiler_params=pltpu.CompilerParams(vmem_limit_bytes=VMEM_LIMIT),
        name="hyena_filter",
    )(z, t, w1p, row(b1), w2, row(b2), w3, row(b3), w4, row(freq), row(jnp.tile(deltas, 2)))


def _hyena(pb, hy, seq):
    conv_w, conv_b, w1, b1, w2, b2, w3, b3, w4, freq, d_skip = hy
    taps, inv_norm = _hy_filter(seq, w1, b1, w2, b2, w3, b3, w4, freq)
    h_spec = _hy_filter_spec(_hy_stage1(taps, seq, False), inv_norm, seq)
    u, x0c = _hy_prep(pb, conv_w, conv_b)
    y = _hy_out(_hy_mid(_hy_stage1(u, seq, True), h_spec, seq), seq)
    flat = lambda a: a.reshape(-1, HY_WIDTH)
    return flat(y), flat(u), flat(x0c), d_skip.reshape(1, HY_WIDTH)


def _take_cols(w, idx):
    idx = np.asarray(idx)
    return jnp.where(jnp.asarray(idx >= 0)[None, :], w[:, np.maximum(idx, 0)], 0.0)


def _mix_weights(w_in, w_out, a_qn, a_kn, q_g, kv_g, w_uq, w_ukv):
    col = np.arange
    q_cols = np.concatenate([col(h * HEAD_DIM, (h + 1) * HEAD_DIM) for h in A_HEAD_ORDER])
    pad = lambda n: -np.ones(n, np.int64)
    in_idx = np.concatenate([q_cols, col(A_WIDTH, OFF_KR), pad(HEAD_DIM), col(OFF_KR, OFF_KR + C_ROPE),
                             pad(LANES - HEAD_DIM - C_ROPE)])
    dq = C_NOPE + C_ROPE
    uq_idx = np.concatenate([np.concatenate([col(h * dq, (h + 1) * dq), pad(LANES - dq)]) for h in range(C_HEADS)])
    dkv = C_NOPE + C_V
    uk_idx = np.concatenate([np.concatenate([col(h * dkv, h * dkv + C_NOPE), pad(LANES - C_NOPE)])
                             for h in range(C_HEADS)])
    v_cols = lambda h: col(h * dkv + C_NOPE, (h + 1) * dkv)
    uv_idx = np.concatenate([np.concatenate([v_cols(h), pad(LANES - C_V)] if h % 2 == 0 else
                                            [pad(LANES - C_V), v_cols(h)]) for h in range(C_HEADS)])
    half = np.arange(LANES) // HEAD_DIM
    e = (half[:, None] == half[None, :]).astype(np.float32)
    tile2 = lambda g: jnp.tile(g, LANES // HEAD_DIM).reshape(1, LANES)
    return dict(
        w_in=_take_cols(w_in, in_idx).astype(BF16),
        e=jnp.asarray(e, BF16),
        gq=tile2(a_qn), gk=tile2(a_kn),
        cq_g=q_g.reshape(1, -1), ckv_g=kv_g.reshape(1, -1),
        w_uq=_take_cols(w_uq, uq_idx).astype(BF16),
        w_uk=_take_cols(w_ukv, uk_idx).astype(BF16),
        w_uv=_take_cols(w_ukv, uv_idx).astype(BF16),
        wa=w_out[q_cols].astype(BF16),
        wb=w_out[A_WIDTH:A_WIDTH + HY_WIDTH].astype(BF16),
        wc=w_out[A_WIDTH + HY_WIDTH:].astype(BF16),
    )


def _rope_tables(seq):
    rows = seq // GRID_W
    row = jnp.repeat(jnp.arange(rows, dtype=F32), GRID_W)
    colv = jnp.tile(jnp.arange(GRID_W, dtype=F32), rows)

    def cos_sin(rot_dim):
        n_freq = rot_dim // 4
        inv = ROPE_THETA ** (-jnp.arange(n_freq, dtype=F32) / n_freq)
        ang = jnp.concatenate([row[:, None] * inv, colv[:, None] * inv], -1)
        return jnp.cos(ang), jnp.sin(ang)

    cos, sin = cos_sin(HEAD_DIM)
    cos_a = jnp.tile(cos, (1, 2 * LANES // HEAD_DIM))
    sin_a = jnp.tile(jnp.concatenate([-sin, sin], -1), (1, LANES // HEAD_DIM))
    cos, sin = cos_sin(C_ROPE)
    ones = jnp.ones((seq, C_NOPE), F32)
    tail = LANES - C_NOPE - C_ROPE
    cos_c = jnp.concatenate([ones, cos, cos, jnp.ones((seq, tail), F32)], -1)
    sin_c = jnp.concatenate([0.0 * ones, -sin, sin, jnp.zeros((seq, tail), F32)], -1)
    return cos_a, sin_a, cos_c, sin_c


def kernel(x, c, ctx, c_ctx, ada_w, ada_b, ffn1_w_gu, ffn1_w_down, ffn2_w_gu, ffn2_w_down, ln_g, ln_b, w_in, w_out,
           a_q_norm, a_k_norm, hy_conv_w, hy_conv_b, hy_f_w1, hy_f_b1, hy_f_w2, hy_f_b2, hy_f_w3, hy_f_b3, hy_f_w4,
           hy_f_freq, hy_bias, mla_q_norm, mla_kv_norm, mla_w_uq, mla_w_ukv):
    b, seq, d = x.shape
    n_ctx = ctx.shape[1]
    depth = ada_w.shape[0]
    assert b % 2 == 0 and seq % GRID_W == 0

    tm = min(ROW_TILE, seq)
    tm_ctx = min(ROW_TILE, b * n_ctx)
    tiles_per_seq = seq // tm
    lat_batch = lambda i: i // tiles_per_seq
    tm_mix = min(MIX_ROW_TILE, seq)
    mix_tiles_per_seq = seq // tm_mix
    mix_batch = lambda i: i // mix_tiles_per_seq
    ctx_batch = lambda i: b

    mod_rows = ((b + 1 + 7) // 8) * 8
    c_all = jnp.concatenate([c, c_ctx[None], jnp.zeros((mod_rows - b - 1, d), F32)], 0)
    mods = _ada_mod(c_all, ada_w, ada_b).reshape(depth, mod_rows, N_MOD // 3, 3, d)

    rope_tabs = _rope_tables(seq)
    x2 = x.reshape(b * seq, d)
    ctx2 = ctx.reshape(b * n_ctx, d)

    for l in range(depth):
        need_ctx = l < depth - 1
        m4 = mods[l]
        ffn1 = (ffn1_w_gu[l, :, :D_FF].astype(BF16), ffn1_w_gu[l, :, D_FF:].astype(BF16),
                ffn1_w_down[l].astype(BF16))
        ffn2 = (ffn2_w_gu[l, :, :D_FF].astype(BF16), ffn2_w_gu[l, :, D_FF:].astype(BF16),
                ffn2_w_down[l].astype(BF16))
        wts = _mix_weights(w_in[l], w_out[l], a_q_norm[l], a_k_norm[l], mla_q_norm[l], mla_kv_norm[l],
                           mla_w_uq[l], mla_w_ukv[l])
        hy = (hy_conv_w[l], hy_conv_b[l], hy_f_w1[l], hy_f_b1[l], hy_f_w2[l], hy_f_b2[l], hy_f_w3[l], hy_f_b3[l],
              hy_f_w4[l], hy_f_freq[l], hy_bias[l])

        x2 = _ffn_sublayer(x2, m4, 0, lat_batch, *ffn1, ln_g[l, 0], ln_b[l, 0], tm)
        ctx2 = _ffn_sublayer(ctx2, m4, 0, ctx_batch, *ffn1, ln_g[l, 0], ln_b[l, 0], tm_ctx)

        qa, ka, va, pb, qc, kc, vc = _mix_in(x2, m4, mix_batch, wts, rope_tabs, mix_tiles_per_seq, tm_mix)
        qa_c, ka_c, va_c, pb_c, qc_c, kc_c, vc_c = _mix_in(ctx2, m4, ctx_batch, wts, None, 1, tm_ctx)
        r3 = lambda a, n: a.reshape(b, n, a.shape[-1])
        ka_c, va_c, kc_c, vc_c = (r3(a, n_ctx) for a in (ka_c, va_c, kc_c, vc_c))

        oa = _gqa_attention(r3(qa, seq), [(ka_c, va_c), (r3(ka, seq), r3(va, seq))], *ATTN_TILES_A)
        oc = _mla_attention(r3(qc, seq), [(kc_c, vc_c), (r3(kc, seq), r3(vc, seq))], *ATTN_TILES_C)
        hy_parts = _hyena(r3(pb, seq), hy, seq)
        flat = lambda a: a.reshape(-1, a.shape[-1])
        out_w = (wts["wa"], wts["wb"], wts["wc"], ln_g[l, 1], ln_b[l, 1])
        if need_ctx:
            oa_c = _gqa_attention(r3(qa_c, n_ctx), [(ka_c, va_c)], *ATTN_TILES_A)
            oc_c = _mla_attention(r3(qc_c, n_ctx), [(kc_c, vc_c)], *ATTN_TILES_C)
            hy_parts_c = _hyena(r3(pb_c, n_ctx), hy, n_ctx)
            ctx2 = _ffn_sublayer(ctx2, m4, 2, ctx_batch, *ffn2, ln_g[l, 2], ln_b[l, 2], tm_ctx,
                                 mix=(flat(oa_c), hy_parts_c, flat(oc_c), *out_w))
        x2 = _ffn_sublayer(x2, m4, 2, lat_batch, *ffn2, ln_g[l, 2], ln_b[l, 2], tm,
                           mix=(flat(oa), hy_parts, flat(oc), *out_w))
    return x2.reshape(b, seq, d)
```

```python
import functools
import math

import numpy as np
import jax
import jax.numpy as jnp
from jax import lax
from jax.experimental import pallas as pl
from jax.experimental.pallas import tpu as pltpu

F32 = jnp.float32
BF16 = jnp.bfloat16

D_MODEL = 1024
GRID_W = 64
N_MOD = 9
D_FF = 2816
DEPTH = 2
ALPHA = (2 * DEPTH) ** 0.25
EPS = 1e-6
A_HEADS = 6
A_KV_HEADS = 2
HEAD_DIM = 64
ROPE_THETA = 10000.0
HY_WIDTH = 256
HY_EMB = 33
HY_BANDS = (HY_EMB - 1) // 2
HY_ORDER = 64
HY_TARGET = 1e-2
HY_FAST = 0.3
HY_SLOW = 1.5
C_HEADS = 6
C_Q_LORA = 256
C_KV_LORA = 128
C_NOPE = 64
C_ROPE = 32
C_V = 64
A_WIDTH = A_HEADS * HEAD_DIM
C_WIDTH = C_HEADS * C_V
A_COLS = (A_HEADS + 2 * A_KV_HEADS) * HEAD_DIM
B_COLS = 3 * HY_WIDTH
C_COLS = C_Q_LORA + C_KV_LORA + C_ROPE

LANES = 128
MXU_DIM = 256
VMEM_LIMIT = 56 * 1024 * 1024

ROW_TILE = 512
MIX_ROW_TILE = 1024
HY_COL_TILE = 8192
HY_PLANES = 16

P_COLS = A_COLS + B_COLS + C_Q_LORA + C_KV_LORA + LANES
OFF_K = A_WIDTH
OFF_V = A_WIDTH + A_KV_HEADS * HEAD_DIM
OFF_B = A_COLS
OFF_CQ = A_COLS + B_COLS
OFF_CKV = OFF_CQ + C_Q_LORA
OFF_KR = OFF_CKV + C_KV_LORA
A_HEAD_ORDER = (0, 3, 1, 4, 2, 5)
CQ_PAD = C_HEADS * LANES


def _cparams(sem):
    return pltpu.CompilerParams(dimension_semantics=sem, vmem_limit_bytes=VMEM_LIMIT)


def _layer_norm(x):
    mu = jnp.mean(x, -1, keepdims=True)
    xc = x - mu
    var = jnp.mean(xc * xc, -1, keepdims=True)
    return xc * lax.rsqrt(var + EPS)


def _silu(x):
    return x * jax.nn.sigmoid(x)


def _dot(a, b):
    return jnp.dot(a, b, preferred_element_type=F32)


def _dot_nt(a, b):
    return lax.dot_general(a, b, (((1,), (1,)), ((), ())), preferred_element_type=F32)


def _dot_hp(a, b):
    return jnp.dot(a, b, preferred_element_type=F32, precision=lax.Precision.HIGHEST)


def _ada_kernel(c_ref, w_ref, b_ref, o_ref):
    a = _silu(c_ref[...]).astype(BF16)
    o_ref[0] = _dot(a, w_ref[0].astype(BF16)) + b_ref[0]


def _ada_mod(c_all, ada_w, ada_b):
    depth, d, n = ada_w.shape
    rows = c_all.shape[0]
    tn = n // 8
    return pl.pallas_call(
        _ada_kernel,
        grid=(depth, n // tn),
        in_specs=[
            pl.BlockSpec((rows, d), lambda l, j: (0, 0)),
            pl.BlockSpec((1, d, tn), lambda l, j: (l, 0, j)),
            pl.BlockSpec((1, 1, tn), lambda l, j: (l, 0, j)),
        ],
        out_specs=pl.BlockSpec((1, rows, tn), lambda l, j: (l, 0, j)),
        out_shape=jax.ShapeDtypeStruct((depth, rows, n), F32),
        compiler_params=_cparams(("arbitrary", "arbitrary")),
        name="ada_mod",
    )(c_all, ada_w, ada_b.reshape(depth, 1, n))


def _ffn_kernel(*refs, bounds, fused_mix):
    if fused_mix:
        (xin_ref, modm_ref, oa_ref, y_ref, u_ref, x0_ref, d_ref, oc_ref, wa_ref, wb_ref, wc_ref, g1_ref, b1_ref,
         mod_ref, wg_ref, wu_ref, wd_ref, g_ref, b_ref, o_ref, x_ref) = refs
    else:
        x_ref, mod_ref, wg_ref, wu_ref, wd_ref, g_ref, b_ref, o_ref = refs
    mod = mod_ref[0, 0]
    half = x_ref.shape[0] // 2
    halves = (slice(0, half), slice(half, 2 * half))
    n_chunks = len(bounds) - 1

    def mix_rows(r):
        ob = (y_ref[r, :].astype(F32) + u_ref[r, :].astype(F32) * d_ref[...]) * x0_ref[r, :].astype(F32)
        mix = (_dot(oa_ref[r, :], wa_ref[...]) + _dot(ob.astype(BF16), wb_ref[...])
               + _dot(oc_ref[r, :], wc_ref[...]))
        y = ALPHA * xin_ref[r, :] + modm_ref[0, 0][2:3] * mix
        x_ref[r, :] = _layer_norm(y) * g1_ref[...] + b1_ref[...]

    hs = []
    for r in halves:
        if fused_mix:
            mix_rows(r)
        hs.append((_layer_norm(x_ref[r, :]) * (1.0 + mod[1:2]) + mod[0:1]).astype(BF16))
    acc = None
    for k, (lo, hi) in enumerate(zip(bounds[:-1], bounds[1:])):
        if k == 0:
            a = jnp.concatenate([(_silu(_dot(hr, wg_ref[:, lo:hi])) * _dot(hr, wu_ref[:, lo:hi])).astype(BF16)
                                 for hr in hs], axis=0)
            h = jnp.concatenate(hs, axis=0)
        else:
            a = (_silu(_dot(h, wg_ref[:, lo:hi])) * _dot(h, wu_ref[:, lo:hi])).astype(BF16)
        if k == n_chunks - 1:
            for r in halves:
                part = _dot(a[r, :], wd_ref[lo:hi, :])
                y = ALPHA * x_ref[r, :] + (0.5 * mod[2:3]) * (part if acc is None else acc[r, :] + part)
                o_ref[r, :] = _layer_norm(y) * g_ref[...] + b_ref[...]
        else:
            part = _dot(a, wd_ref[lo:hi, :])
            acc = part if acc is None else acc + part


def _ffn_sublayer(x2, mods4, mod_group, batch_of_tile, wg, wu, wd, g, b, tm, mix=None):
    rows, d = x2.shape
    f = wg.shape[1]
    const = lambda i: (0, 0)
    row_spec = lambda w: pl.BlockSpec((tm, w), lambda i: (i, 0))
    mod_spec = lambda group: pl.BlockSpec((1, 1, 3, d), lambda i: (batch_of_tile(i), group, 0, 0))
    vec_spec = pl.BlockSpec((1, d), const)
    n_tiles = -(-f // MXU_DIM)
    bounds = (0, min(f, ((n_tiles + 1) // 2) * MXU_DIM), f)
    in_specs, args, scratch = [row_spec(d)], [x2], []
    if mix is not None:
        oa, (y, u, x0c, d_skip), oc, wa, wb, wc, g1, b1 = mix
        in_specs += [mod_spec(1), row_spec(A_WIDTH), row_spec(HY_WIDTH), row_spec(HY_WIDTH), row_spec(HY_WIDTH),
                     pl.BlockSpec((1, HY_WIDTH), const), row_spec(C_WIDTH),
                     pl.BlockSpec((A_WIDTH, d), const), pl.BlockSpec((HY_WIDTH, d), const),
                     pl.BlockSpec((C_WIDTH, d), const), vec_spec, vec_spec]
        args += [mods4, oa, y, u, x0c, d_skip, oc, wa, wb, wc, g1.reshape(1, d), b1.reshape(1, d)]
        scratch = [pltpu.VMEM((tm, d), F32)]
    in_specs += [mod_spec(mod_group),
                 pl.BlockSpec((d, f), const, pipeline_mode=pl.Buffered(1)),
                 pl.BlockSpec((d, f), const, pipeline_mode=pl.Buffered(1)),
                 pl.BlockSpec((f, d), const, pipeline_mode=pl.Buffered(1)),
                 vec_spec, vec_spec]
    args += [mods4, wg, wu, wd, g.reshape(1, d), b.reshape(1, d)]
    return pl.pallas_call(
        functools.partial(_ffn_kernel, bounds=bounds, fused_mix=mix is not None),
        grid=(rows // tm,),
        in_specs=in_specs,
        out_specs=row_spec(d),
        out_shape=jax.ShapeDtypeStruct((rows, d), F32),
        scratch_shapes=scratch,
        compiler_params=_cparams(("arbitrary",)),
        name="mix_out_ffn" if mix is not None else "ffn_sublayer",
    )(*args)


def _rope(x, cos, sin_signed, half):
    lane = lax.broadcasted_iota(jnp.int32, x.shape, 1)
    first = (lane % (2 * half)) < half
    partner = jnp.where(first, pltpu.roll(x, LANES - half, 1), pltpu.roll(x, half, 1))
    return x * cos + partner * sin_signed


def _half_tile_mean_sq(x, e_ref):
    return _dot((x * x).astype(BF16), e_ref[...]) * (1.0 / HEAD_DIM)


def _mix_in_kernel(x_ref, mod_ref, w_ref, e_ref, gq_ref, gk_ref, cq_g_ref, ckv_g_ref, wuq_ref, wuk_ref, wuv_ref,
                   *rest, use_rope):
    if use_rope:
        cos_a_ref, sin_a_ref, cos_c_ref, sin_c_ref = rest[:4]
        rest = rest[4:]
    qa_ref, ka_ref, va_ref, pb_ref, qc_ref, kc_ref, vc_ref = rest

    mod = mod_ref[0, 0]
    half = x_ref.shape[0] // 2
    scale_c = LOG2E * (C_NOPE + C_ROPE) ** -0.5
    for r in (slice(0, half), slice(half, 2 * half)):
        h = (_layer_norm(x_ref[r, :]) * (1.0 + mod[1:2]) + mod[0:1]).astype(BF16)
        p = _dot(h, w_ref[...])

        def rope_a(t):
            return _rope(t, cos_a_ref[r, :], sin_a_ref[r, :], HEAD_DIM // 2) if use_rope else t

        def rope_c(t):
            return _rope(t, cos_c_ref[r, :], sin_c_ref[r, :], C_ROPE // 2) if use_rope else t

        for c in range(A_WIDTH // LANES):
            t = p[:, c * LANES:(c + 1) * LANES]
            t = t * lax.rsqrt(_half_tile_mean_sq(t, e_ref) + EPS) * gq_ref[...]
            qa_ref[r, c * LANES:(c + 1) * LANES] = (rope_a(t) * (LOG2E * HEAD_DIM ** -0.5)).astype(BF16)
        t = p[:, OFF_K:OFF_K + LANES]
        t = t * lax.rsqrt(_half_tile_mean_sq(t, e_ref) + EPS) * gk_ref[...]
        ka_ref[r, :] = rope_a(t).astype(BF16)
        v = p[:, OFF_V:OFF_V + LANES]
        low = lax.broadcasted_iota(jnp.int32, v.shape, 1) < HEAD_DIM
        va_ref[r, 0:LANES] = jnp.where(low, v, 1.0).astype(BF16)
        va_ref[r, LANES:2 * LANES] = jnp.where(low, 1.0, v).astype(BF16)

        pb_ref[r, :] = p[:, OFF_B:OFF_B + B_COLS].astype(BF16)

        c_q = p[:, OFF_CQ:OFF_CQ + C_Q_LORA]
        c_q = c_q * lax.rsqrt(jnp.mean(c_q * c_q, -1, keepdims=True) + EPS) * cq_g_ref[...]
        qq = _dot(c_q.astype(BF16), wuq_ref[...])
        c_kv = p[:, OFF_CKV:OFF_CKV + C_KV_LORA]
        c_kv = (c_kv * lax.rsqrt(jnp.mean(c_kv * c_kv, -1, keepdims=True) + EPS) * ckv_g_ref[...]).astype(BF16)
        kn = _dot(c_kv, wuk_ref[...])
        vv = _dot(c_kv, wuv_ref[...])
        k_r = rope_c(p[:, OFF_KR:OFF_KR + LANES])
        for hd in range(C_HEADS):
            sl = slice(hd * LANES, (hd + 1) * LANES)
            qc_ref[r, sl] = (rope_c(qq[:, sl]) * scale_c).astype(BF16)
            kc_ref[r, sl] = (kn[:, sl] + k_r).astype(BF16)
            ones_low = (hd % 2) == 1
            vc_ref[r, sl] = jnp.where(low != ones_low, vv[:, sl], 1.0).astype(BF16)


def _mix_in(x2, mods4, batch_of_tile, wts, rope_tabs, tiles_per_seq, tm):
    rows, d = x2.shape
    const = lambda i: (0, 0)
    use_rope = rope_tabs is not None
    in_specs = [
        pl.BlockSpec((tm, d), lambda i: (i, 0)),
        pl.BlockSpec((1, 1, 3, d), lambda i: (batch_of_tile(i), 1, 0, 0)),
        pl.BlockSpec((d, P_COLS), const, pipeline_mode=pl.Buffered(1)),
        pl.BlockSpec((LANES, LANES), const),
        pl.BlockSpec((1, LANES), const),
        pl.BlockSpec((1, LANES), const),
        pl.BlockSpec((1, C_Q_LORA), const),
        pl.BlockSpec((1, C_KV_LORA), const),
        pl.BlockSpec((C_Q_LORA, CQ_PAD), const),
        pl.BlockSpec((C_KV_LORA, CQ_PAD), const),
        pl.BlockSpec((C_KV_LORA, CQ_PAD), const),
    ]
    args = [x2, mods4, wts["w_in"], wts["e"], wts["gq"], wts["gk"], wts["cq_g"], wts["ckv_g"],
            wts["w_uq"], wts["w_uk"], wts["w_uv"]]
    if use_rope:
        in_specs += [pl.BlockSpec((tm, LANES), lambda i: (i % tiles_per_seq, 0))] * 4
        args += list(rope_tabs)
    row_spec = lambda w: pl.BlockSpec((tm, w), lambda i: (i, 0))
    widths = (A_WIDTH, LANES, 2 * LANES, B_COLS, CQ_PAD, CQ_PAD, CQ_PAD)
    return pl.pallas_call(
        functools.partial(_mix_in_kernel, use_rope=use_rope),
        grid=(rows // tm,),
        in_specs=in_specs,
        out_specs=[row_spec(w) for w in widths],
        out_shape=[jax.ShapeDtypeStruct((rows, w), BF16) for w in widths],
        compiler_params=_cparams(("arbitrary",)),
        name="mix_in_rope" if use_rope else "mix_in",
    )(*args)


ATTN_TILES_A = (256, 512, 2)
ATTN_TILES_C = (512, 512)
LOG2E = math.log2(math.e)


def _key_chunks(kv_refs, tk):
    chunks = []
    for s in range(len(kv_refs) // 2):
        n_rows = kv_refs[2 * s].shape[1]
        size = min(tk, n_rows)
        chunks += [(s, off, size) for off in range(0, n_rows, size)]
    return chunks


def _flash(q_get, n_chain, chunks, k_get, v_get, scratch):
    s_refs = [scratch[2 * c:2 * c + 2] for c in range(n_chain)]
    m_refs = scratch[2 * n_chain:3 * n_chain]
    acc_refs = scratch[3 * n_chain:4 * n_chain]

    def scores(i):
        size = chunks[i][2]
        for c in range(n_chain):
            s_refs[c][i % 2][:, 0:size] = _dot_nt(q_get(c), k_get(c, chunks[i]))

    def softmax_pv(i):
        size = chunks[i][2]
        for c in range(n_chain):
            slabs = [s_refs[c][i % 2][:, j * LANES:(j + 1) * LANES] for j in range(size // LANES)]
            m_new = functools.reduce(jnp.maximum, slabs)
            m_new = jnp.broadcast_to(jnp.max(m_new, axis=-1, keepdims=True), m_new.shape)
            if i > 0:
                m_old = m_refs[c][...]
                m_new = jnp.maximum(m_old, m_new)
            p = jnp.concatenate([jnp.exp2(sl - m_new).astype(BF16) for sl in slabs], axis=1)
            pv = _dot(p, v_get(c, chunks[i]))
            acc_refs[c][...] = pv if i == 0 else jnp.exp2(m_old - m_new) * acc_refs[c][...] + pv
            m_refs[c][...] = m_new

    scores(0)
    for i in range(len(chunks)):
        if i + 1 < len(chunks):
            scores(i + 1)
        softmax_pv(i)
    return acc_refs


def _attn_scratch(n_chain, m_rows, tk):
    return ([pltpu.VMEM((m_rows, tk), F32)] * (2 * n_chain) + [pltpu.VMEM((m_rows, LANES), F32)] * (2 * n_chain))


def _merge_halves(acc_low, acc_high):
    lane = lax.broadcasted_iota(jnp.int32, acc_low.shape, 1)
    low = lane < HEAD_DIM
    num = jnp.where(low, acc_low, acc_high)
    den = pltpu.roll(jnp.where(low, acc_high, acc_low), HEAD_DIM, 1)
    return num / den


def _gqa_kernel(q_ref, *refs, n_src, tk, n_sub):
    kv_refs, o_ref = refs[:2 * n_src], refs[2 * n_src]
    scratch = refs[2 * n_src + 1:]
    per_sub = len(scratch) // n_sub
    tq = q_ref.shape[1] // n_sub
    n_tiles = A_WIDTH // LANES
    low = lax.broadcasted_iota(jnp.int32, (tq, LANES), 1) < HEAD_DIM
    zero = jnp.zeros((tq, LANES), BF16)
    k_get = lambda g, ch: kv_refs[2 * ch[0]][0, ch[1]:ch[1] + ch[2], :]
    v_get = lambda g, ch: kv_refs[2 * ch[0] + 1][0, ch[1]:ch[1] + ch[2], g * LANES:(g + 1) * LANES]
    for t in range(n_sub):
        qs_ref, sub_scratch = scratch[t * per_sub], scratch[t * per_sub + 1:(t + 1) * per_sub]
        q_rows = slice(t * tq, (t + 1) * tq)
        for g in range(A_KV_HEADS):
            for c in range(n_tiles):
                qs_ref[g, c * tq:(c + 1) * tq, :] = jnp.where(
                    low == (g == 0), q_ref[0, q_rows, c * LANES:(c + 1) * LANES], zero)
        q_get = lambda g, qs_ref=qs_ref: qs_ref[g]
        acc = _flash(q_get, A_KV_HEADS, _key_chunks(kv_refs, tk), k_get, v_get, sub_scratch)
        for c in range(n_tiles):
            rows = slice(c * tq, (c + 1) * tq)
            o_ref[0, q_rows, c * LANES:(c + 1) * LANES] = _merge_halves(acc[0][rows, :], acc[1][rows, :]).astype(BF16)


def _gqa_attention(q, kv_list, tq, tk, n_sub):
    b, lq, _ = q.shape
    tk = min(tk, max(k.shape[1] for k, _ in kv_list))
    if lq < tq * n_sub:
        tq, n_sub = min(tq, lq), 1
    m_rows = (A_HEADS // A_KV_HEADS) * tq
    tq = tq * n_sub
    in_specs = [pl.BlockSpec((1, tq, A_WIDTH), lambda i, j: (i, j, 0))]
    args = [q]
    for k, v in kv_list:
        in_specs.append(pl.BlockSpec((1, k.shape[1], LANES), lambda i, j: (i, 0, 0)))
        in_specs.append(pl.BlockSpec((1, v.shape[1], 2 * LANES), lambda i, j: (i, 0, 0)))
        args += [k, v]
    return pl.pallas_call(
        functools.partial(_gqa_kernel, n_src=len(kv_list), tk=tk, n_sub=n_sub),
        grid=(b, lq // tq),
        in_specs=in_specs,
        out_specs=pl.BlockSpec((1, tq, A_WIDTH), lambda i, j: (i, j, 0)),
        out_shape=jax.ShapeDtypeStruct((b, lq, A_WIDTH), BF16),
        scratch_shapes=([pltpu.VMEM((A_KV_HEADS, m_rows, LANES), BF16)]
                        + _attn_scratch(A_KV_HEADS, m_rows, tk)) * n_sub,
        compiler_params=_cparams(("arbitrary", "arbitrary")),
        name="gqa_attention",
    )(*args)


def _mla_kernel(q_ref, *refs, n_src, tk):
    kv_refs, o_ref = refs[:2 * n_src], refs[2 * n_src]
    scratch = refs[2 * n_src + 1:]
    n_pairs = C_HEADS // 2
    per_pair = len(scratch) // n_pairs
    for hp in range(n_pairs):
        lanes = lambda e, hp=hp: slice((2 * hp + e) * LANES, (2 * hp + e + 1) * LANES)
        q_get = lambda e, lanes=lanes: q_ref[0, :, lanes(e)]
        k_get = lambda e, ch, lanes=lanes: kv_refs[2 * ch[0]][0, ch[1]:ch[1] + ch[2], lanes(e)]
        v_get = lambda e, ch, lanes=lanes: kv_refs[2 * ch[0] + 1][0, ch[1]:ch[1] + ch[2], lanes(e)]
        acc = _flash(q_get, 2, _key_chunks(kv_refs, tk), k_get, v_get, scratch[hp * per_pair:(hp + 1) * per_pair])
        o_ref[0, :, hp * LANES:(hp + 1) * LANES] = _merge_halves(acc[0][...], acc[1][...]).astype(BF16)


def _mla_attention(q, kv_list, tq, tk):
    b, lq, _ = q.shape
    tq, tk = min(tq, lq), min(tk, max(k.shape[1] for k, _ in kv_list))
    in_specs = [pl.BlockSpec((1, tq, CQ_PAD), lambda i, j: (i, j, 0))]
    args = [q]
    for k, v in kv_list:
        in_specs.append(pl.BlockSpec((1, k.shape[1], CQ_PAD), lambda i, j: (i, 0, 0)))
        in_specs.append(pl.BlockSpec((1, v.shape[1], CQ_PAD), lambda i, j: (i, 0, 0)))
        args += [k, v]
    return pl.pallas_call(
        functools.partial(_mla_kernel, n_src=len(kv_list), tk=tk),
        grid=(b, lq // tq),
        in_specs=in_specs,
        out_specs=pl.BlockSpec((1, tq, C_WIDTH), lambda i, j: (i, j, 0)),
        out_shape=jax.ShapeDtypeStruct((b, lq, C_WIDTH), BF16),
        scratch_shapes=_attn_scratch(2, tq, tk) * (C_HEADS // 2),
        compiler_params=_cparams(("arbitrary", "arbitrary")),
        name="mla_attention",
    )(*args)


def _fft_dims(seq):
    n = 2 * seq
    n2 = min(LANES, n // 16)
    return n // n2, n2


def _stack_real(m):
    return np.block([[m.real, -m.imag], [m.imag, m.real]]).astype(np.float32)


@functools.lru_cache(maxsize=None)
def _fft_constants(seq):
    n1, n2 = _fft_dims(seq)
    n = n1 * n2
    k1 = np.arange(n1)[:, None]
    f1 = np.exp(-2j * np.pi * k1 * np.arange(n1 // 2)[None, :] / n1)
    f2 = np.exp(-2j * np.pi * np.arange(n2)[:, None] * np.arange(n2)[None, :] / n2)
    tw = np.exp(-2j * np.pi * k1 * np.arange(n2)[None, :] / n)
    g1 = np.conj(f1).T / n
    return dict(
        m1=_stack_real(f1), m2=_stack_real(f2), m2_inv=_stack_real(np.conj(f2)), m1_inv=_stack_real(g1),
        tw_r=tw.real.astype(np.float32)[:, :, None], tw_i=tw.imag.astype(np.float32)[:, :, None])


def _hy_prep_kernel(v_ref, x1_ref, x0_ref, wv_ref, wx1_ref, wx0_ref, bv_ref, bx1_ref, bx0_ref, u_ref, x0c_ref):
    seq = v_ref.shape[1]
    row = lax.broadcasted_iota(jnp.int32, (seq, LANES), 0)
    first, last = row == 0, row == seq - 1

    def conv(p_ref, w_ref, b_ref):
        t = p_ref[0].astype(F32)
        prev = jnp.where(first, 0.0, pltpu.roll(t, 1, 0))
        nxt = jnp.where(last, 0.0, pltpu.roll(t, seq - 1, 0))
        return prev * w_ref[0:1] + t * w_ref[1:2] + nxt * w_ref[2:3] + b_ref[...]

    u_ref[0] = (conv(v_ref, wv_ref, bv_ref) * conv(x1_ref, wx1_ref, bx1_ref)).astype(BF16)
    x0c_ref[0] = conv(x0_ref, wx0_ref, bx0_ref).astype(BF16)


def _hy_prep(pb, conv_w, conv_b):
    b, seq, _ = pb.shape
    n_c = HY_WIDTH // LANES
    p_spec = lambda part: pl.BlockSpec((1, seq, LANES), lambda i, c: (i, 0, part * n_c + c))
    w_spec = lambda part: pl.BlockSpec((3, LANES), lambda i, c: (0, part * n_c + c))
    b_spec = lambda part: pl.BlockSpec((1, LANES), lambda i, c: (0, part * n_c + c))
    out_spec = pl.BlockSpec((1, seq, LANES), lambda i, c: (i, 0, c))
    cb = conv_b.reshape(1, B_COLS)
    return pl.pallas_call(
        _hy_prep_kernel,
        grid=(b, n_c),
        in_specs=[p_spec(0), p_spec(1), p_spec(2), w_spec(0), w_spec(1), w_spec(2), b_spec(0), b_spec(1), b_spec(2)],
        out_specs=[out_spec, out_spec],
        out_shape=[jax.ShapeDtypeStruct((b, seq, HY_WIDTH), BF16)] * 2,
        compiler_params=_cparams(("arbitrary", "arbitrary")),
        name="hyena_prep",
    )(pb, pb, pb, conv_w, conv_w, conv_w, cb, cb, cb)


def _hy_stage1_kernel(u_ref, m1_ref, a_ref):
    members, h, lc = u_ref.shape
    z = u_ref[...].astype(F32).reshape(members * h, lc).astype(BF16)
    a = _dot(m1_ref[...].astype(BF16), z)
    a_ref[0] = a.reshape(2, a.shape[0] // 2, lc // HY_WIDTH, HY_WIDTH).astype(BF16)


def _hy_stage1(u, seq, paired):
    b, _, ch = u.shape
    n1, n2 = _fft_dims(seq)
    cols = n2 * ch
    lc = min(cols, HY_COL_TILE)
    members = 2 if paired else 1
    m1 = _fft_constants(seq)["m1"]
    m1 = jnp.asarray(m1 if paired else m1[:, :n1 // 2])
    return pl.pallas_call(
        _hy_stage1_kernel,
        grid=(b // members, cols // lc),
        in_specs=[pl.BlockSpec((members, n1 // 2, lc), lambda j, c: (j, 0, c)),
                  pl.BlockSpec(m1.shape, lambda j, c: (0, 0))],
        out_specs=pl.BlockSpec((1, 2, n1, lc // ch, ch), lambda j, c: (j, 0, 0, c, 0)),
        out_shape=jax.ShapeDtypeStruct((b // members, 2, n1, n2, ch), BF16),
        compiler_params=_cparams(("arbitrary", "arbitrary")),
        name="hyena_stage1",
    )(u.reshape(b, n1 // 2, cols), m1)


def _twiddle(ar, ai, tr, ti):
    return ar * tr - ai * ti, ar * ti + ai * tr


def _hy_filter_spec_kernel(a_ref, twr_ref, twi_ref, m2_ref, inv_ref, h_ref):
    n2 = a_ref.shape[3]
    m2 = m2_ref[...].astype(BF16)
    for p in range(a_ref.shape[2]):
        tr, ti = twr_ref[p], twi_ref[p]

        def fwd(seq_idx):
            br, bi = _twiddle(a_ref[seq_idx, 0, p].astype(F32), a_ref[seq_idx, 1, p].astype(F32), tr, ti)
            return _dot(m2, jnp.concatenate([br, bi], axis=0).astype(BF16))

        zf, zb = fwd(0), fwd(1)
        h_ref[p, 0:n2] = (zf[0:n2] + zb[0:n2]) * inv_ref[...]
        h_ref[p, n2:2 * n2] = (zf[n2:2 * n2] - zb[n2:2 * n2]) * inv_ref[...]


def _hy_filter_spec(a, inv_norm, seq):
    n1, n2 = _fft_dims(seq)
    cst = _fft_constants(seq)
    kp = min(HY_PLANES, n1)
    a5 = a.reshape(2, 2, n1, n2, HY_WIDTH)
    return pl.pallas_call(
        _hy_filter_spec_kernel,
        grid=(n1 // kp,),
        in_specs=[pl.BlockSpec((2, 2, kp, n2, HY_WIDTH), lambda k: (0, 0, k, 0, 0)),
                  pl.BlockSpec((kp, n2, 1), lambda k: (k, 0, 0)),
                  pl.BlockSpec((kp, n2, 1), lambda k: (k, 0, 0)),
                  pl.BlockSpec((2 * n2, 2 * n2), lambda k: (0, 0)),
                  pl.BlockSpec((1, HY_WIDTH), lambda k: (0, 0))],
        out_specs=pl.BlockSpec((kp, 2 * n2, HY_WIDTH), lambda k: (k, 0, 0)),
        out_shape=jax.ShapeDtypeStruct((n1, 2 * n2, HY_WIDTH), F32),
        compiler_params=_cparams(("arbitrary",)),
        name="hyena_filter_spectrum",
    )(a5, jnp.asarray(cst["tw_r"]), jnp.asarray(cst["tw_i"]), jnp.asarray(cst["m2"]), inv_norm)


def _hy_mid_kernel(a_ref, twr_ref, twi_ref, m2_ref, m2i_ref, h_ref, q_ref):
    n2 = a_ref.shape[3]
    m2, m2i = m2_ref[...].astype(BF16), m2i_ref[...].astype(BF16)
    for p in range(a_ref.shape[2]):
        tr, ti = twr_ref[p], twi_ref[p]
        br, bi = _twiddle(a_ref[0, 0, p].astype(F32), a_ref[0, 1, p].astype(F32), tr, ti)
        z = _dot(m2, jnp.concatenate([br, bi], axis=0).astype(BF16))
        zr, zi = z[0:n2], z[n2:2 * n2]
        hr, hi = h_ref[p, 0:n2], h_ref[p, n2:2 * n2]
        yr, yi = zr * hr - zi * hi, zr * hi + zi * hr
        y = _dot(m2i, jnp.concatenate([yr, yi], axis=0).astype(BF16))
        qr, qi = _twiddle(y[0:n2], y[n2:2 * n2], tr, -ti)
        q_ref[0, 0, p] = qr.astype(BF16)
        q_ref[0, 1, p] = qi.astype(BF16)


def _hy_mid(a, h_spec, seq):
    pairs = a.shape[0]
    n1, n2 = _fft_dims(seq)
    cst = _fft_constants(seq)
    kp = min(HY_PLANES, n1)
    a5 = a.reshape(pairs, 2, n1, n2, HY_WIDTH)
    blk = pl.BlockSpec((1, 2, kp, n2, HY_WIDTH), lambda k, j: (j, 0, k, 0, 0))
    q = pl.pallas_call(
        _hy_mid_kernel,
        grid=(n1 // kp, pairs),
        in_specs=[blk,
                  pl.BlockSpec((kp, n2, 1), lambda k, j: (k, 0, 0)),
                  pl.BlockSpec((kp, n2, 1), lambda k, j: (k, 0, 0)),
                  pl.BlockSpec((2 * n2, 2 * n2), lambda k, j: (0, 0)),
                  pl.BlockSpec((2 * n2, 2 * n2), lambda k, j: (0, 0)),
                  pl.BlockSpec((kp, 2 * n2, HY_WIDTH), lambda k, j: (k, 0, 0))],
        out_specs=blk,
        out_shape=jax.ShapeDtypeStruct((pairs, 2, n1, n2, HY_WIDTH), BF16),
        compiler_params=_cparams(("arbitrary", "arbitrary")),
        name="hyena_mid",
    )(a5, jnp.asarray(cst["tw_r"]), jnp.asarray(cst["tw_i"]), jnp.asarray(cst["m2"]), jnp.asarray(cst["m2_inv"]),
      h_spec)
    return q


def _hy_out_kernel(q_ref, m1i_ref, y_ref):
    _, _, n1, n2c, ch = q_ref.shape
    y = _dot(m1i_ref[...].astype(BF16), q_ref[0].reshape(2 * n1, n2c * ch))
    y_ref[...] = y.reshape(2, n1 // 2, n2c, ch).astype(BF16)


def _hy_out(q, seq):
    pairs, _, n1, n2, ch = q.shape
    cols = n2 * ch
    lc = min(cols, HY_COL_TILE)
    m1i = jnp.asarray(_fft_constants(seq)["m1_inv"])
    y = pl.pallas_call(
        _hy_out_kernel,
        grid=(pairs, cols // lc),
        in_specs=[pl.BlockSpec((1, 2, n1, lc // ch, ch), lambda j, c: (j, 0, 0, c, 0)),
                  pl.BlockSpec((n1, 2 * n1), lambda j, c: (0, 0))],
        out_specs=pl.BlockSpec((2, n1 // 2, lc // ch, ch), lambda j, c: (j, 0, c, 0)),
        out_shape=jax.ShapeDtypeStruct((2 * pairs, n1 // 2, n2, ch), BF16),
        compiler_params=_cparams(("arbitrary", "arbitrary")),
        name="hyena_out",
    )(q, m1i)
    return y.reshape(2 * pairs, seq, HY_WIDTH)


def _hy_filter_kernel(z_ref, t_ref, w1_ref, b1_ref, w2_ref, b2_ref, w3_ref, b3_ref, w4_ref, fr_ref, dl_ref,
                      taps_ref, inv_ref):
    fr = fr_ref[...]
    hdn = jnp.sin(fr * (_dot_hp(z_ref[...], w1_ref[...]) + b1_ref[...]))
    hdn = jnp.sin(fr * (_dot_hp(hdn, w2_ref[...]) + b2_ref[...]))
    hdn = jnp.sin(fr * (_dot_hp(hdn, w3_ref[...]) + b3_ref[...]))
    h = _dot_hp(hdn, w4_ref[...]) * jnp.exp(-t_ref[...] * dl_ref[...])
    h_fwd, h_bwd = h[:, :HY_WIDTH], h[:, HY_WIDTH:]
    row = lax.broadcasted_iota(jnp.int32, h_bwd.shape, 0)
    h_bwd = jnp.where(row == 0, 0.0, h_bwd)
    norm = jnp.sum(jnp.abs(h_fwd), 0, keepdims=True) + jnp.sum(jnp.abs(h_bwd), 0, keepdims=True)
    inv_ref[...] = 1.0 / norm
    taps_ref[0] = h_fwd
    taps_ref[1] = h_bwd


def _hy_filter(seq, w1, b1, w2, b2, w3, b3, w4, freq):
    t = jnp.linspace(0.0, 1.0, seq, dtype=F32)[:, None]
    w = 2.0 * math.pi * jnp.arange(seq, dtype=F32)[:, None] / seq
    f = jnp.linspace(1e-4, HY_BANDS - 1, HY_BANDS, dtype=F32)[None, :]
    z = jnp.concatenate([t, jnp.cos(f * w), -jnp.sin(f * w)], -1)
    z = jnp.pad(z, ((0, 0), (0, HY_ORDER - HY_EMB)))
    w1p = jnp.pad(w1, ((0, HY_ORDER - HY_EMB), (0, 0)))
    deltas = jnp.abs(jnp.linspace(math.log(HY_TARGET) / HY_SLOW, math.log(HY_TARGET) / HY_FAST, HY_WIDTH, dtype=F32))
    row = lambda v: v.reshape(1, -1)
    return pl.pallas_call(
        _hy_filter_kernel,
        out_shape=[jax.ShapeDtypeStruct((2, seq, HY_WIDTH), F32), jax.ShapeDtypeStruct((1, HY_WIDTH), F32)],
        compiler_params=pltpu.CompilerParams(vmem_limit_bytes=VMEM_LIMIT),
        name="hyena_filter",
    )(z, t, w1p, row(b1), w2, row(b2), w3, row(b3), w4, row(freq), row(jnp.tile(deltas, 2)))


def _hyena(pb, hy, seq):
    conv_w, conv_b, w1, b1, w2, b2, w3, b3, w4, freq, d_skip = hy
    taps, inv_norm = _hy_filter(seq, w1, b1, w2, b2, w3, b3, w4, freq)
    h_spec = _hy_filter_spec(_hy_stage1(taps, seq, False), inv_norm, seq)
    u, x0c = _hy_prep(pb, conv_w, conv_b)
    y = _hy_out(_hy_mid(_hy_stage1(u, seq, True), h_spec, seq), seq)
    flat = lambda a: a.reshape(-1, HY_WIDTH)
    return flat(y), flat(u), flat(x0c), d_skip.reshape(1, HY_WIDTH)


def _take_cols(w, idx):
    idx = np.asarray(idx)
    return jnp.where(jnp.asarray(idx >= 0)[None, :], w[:, np.maximum(idx, 0)], 0.0)


def _mix_weights(w_in, w_out, a_qn, a_kn, q_g, kv_g, w_uq, w_ukv):
    col = np.arange
    q_cols = np.concatenate([col(h * HEAD_DIM, (h + 1) * HEAD_DIM) for h in A_HEAD_ORDER])
    pad = lambda n: -np.ones(n, np.int64)
    in_idx = np.concatenate([q_cols, col(A_WIDTH, OFF_KR), pad(HEAD_DIM), col(OFF_KR, OFF_KR + C_ROPE),
                             pad(LANES - HEAD_DIM - C_ROPE)])
    dq = C_NOPE + C_ROPE
    uq_idx = np.concatenate([np.concatenate([col(h * dq, (h + 1) * dq), pad(LANES - dq)]) for h in range(C_HEADS)])
    dkv = C_NOPE + C_V
    uk_idx = np.concatenate([np.concatenate([col(h * dkv, h * dkv + C_NOPE), pad(LANES - C_NOPE)])
                             for h in range(C_HEADS)])
    v_cols = lambda h: col(h * dkv + C_NOPE, (h + 1) * dkv)
    uv_idx = np.concatenate([np.concatenate([v_cols(h), pad(LANES - C_V)] if h % 2 == 0 else
                                            [pad(LANES - C_V), v_cols(h)]) for h in range(C_HEADS)])
    half = np.arange(LANES) // HEAD_DIM
    e = (half[:, None] == half[None, :]).astype(np.float32)
    tile2 = lambda g: jnp.tile(g, LANES // HEAD_DIM).reshape(1, LANES)
    return dict(
        w_in=_take_cols(w_in, in_idx).astype(BF16),
        e=jnp.asarray(e, BF16),
        gq=tile2(a_qn), gk=tile2(a_kn),
        cq_g=q_g.reshape(1, -1), ckv_g=kv_g.reshape(1, -1),
        w_uq=_take_cols(w_uq, uq_idx).astype(BF16),
        w_uk=_take_cols(w_ukv, uk_idx).astype(BF16),
        w_uv=_take_cols(w_ukv, uv_idx).astype(BF16),
        wa=w_out[q_cols].astype(BF16),
        wb=w_out[A_WIDTH:A_WIDTH + HY_WIDTH].astype(BF16),
        wc=w_out[A_WIDTH + HY_WIDTH:].astype(BF16),
    )


def _rope_tables(seq):
    rows = seq // GRID_W
    row = jnp.repeat(jnp.arange(rows, dtype=F32), GRID_W)
    colv = jnp.tile(jnp.arange(GRID_W, dtype=F32), rows)

    def cos_sin(rot_dim):
        n_freq = rot_dim // 4
        inv = ROPE_THETA ** (-jnp.arange(n_freq, dtype=F32) / n_freq)
        ang = jnp.concatenate([row[:, None] * inv, colv[:, None] * inv], -1)
        return jnp.cos(ang), jnp.sin(ang)

    cos, sin = cos_sin(HEAD_DIM)
    cos_a = jnp.tile(cos, (1, 2 * LANES // HEAD_DIM))
    sin_a = jnp.tile(jnp.concatenate([-sin, sin], -1), (1, LANES // HEAD_DIM))
    cos, sin = cos_sin(C_ROPE)
    ones = jnp.ones((seq, C_NOPE), F32)
    tail = LANES - C_NOPE - C_ROPE
    cos_c = jnp.concatenate([ones, cos, cos, jnp.ones((seq, tail), F32)], -1)
    sin_c = jnp.concatenate([0.0 * ones, -sin, sin, jnp.zeros((seq, tail), F32)], -1)
    return cos_a, sin_a, cos_c, sin_c


def kernel(x, c, ctx, c_ctx, ada_w, ada_b, ffn1_w_gu, ffn1_w_down, ffn2_w_gu, ffn2_w_down, ln_g, ln_b, w_in, w_out,
           a_q_norm, a_k_norm, hy_conv_w, hy_conv_b, hy_f_w1, hy_f_b1, hy_f_w2, hy_f_b2, hy_f_w3, hy_f_b3, hy_f_w4,
           hy_f_freq, hy_bias, mla_q_norm, mla_kv_norm, mla_w_uq, mla_w_ukv):
    b, seq, d = x.shape
    n_ctx = ctx.shape[1]
    depth = ada_w.shape[0]
    assert b % 2 == 0 and seq % GRID_W == 0

    tm = min(ROW_TILE, seq)
    tm_ctx = min(ROW_TILE, b * n_ctx)
    tiles_per_seq = seq // tm
    lat_batch = lambda i: i // tiles_per_seq
    tm_mix = min(MIX_ROW_TILE, seq)
    mix_tiles_per_seq = seq // tm_mix
    mix_batch = lambda i: i // mix_tiles_per_seq
    ctx_batch = lambda i: b

    mod_rows = ((b + 1 + 7) // 8) * 8
    c_all = jnp.concatenate([c, c_ctx[None], jnp.zeros((mod_rows - b - 1, d), F32)], 0)
    mods = _ada_mod(c_all, ada_w, ada_b).reshape(depth, mod_rows, N_MOD // 3, 3, d)

    rope_tabs = _rope_tables(seq)
    x2 = x.reshape(b * seq, d)
    ctx2 = ctx.reshape(b * n_ctx, d)

    for l in range(depth):
        need_ctx = l < depth - 1
        m4 = mods[l]
        ffn1 = (ffn1_w_gu[l, :, :D_FF].astype(BF16), ffn1_w_gu[l, :, D_FF:].astype(BF16),
                ffn1_w_down[l].astype(BF16))
        ffn2 = (ffn2_w_gu[l, :, :D_FF].astype(BF16), ffn2_w_gu[l, :, D_FF:].astype(BF16),
                ffn2_w_down[l].astype(BF16))
        wts = _mix_weights(w_in[l], w_out[l], a_q_norm[l], a_k_norm[l], mla_q_norm[l], mla_kv_norm[l],
                           mla_w_uq[l], mla_w_ukv[l])
        hy = (hy_conv_w[l], hy_conv_b[l], hy_f_w1[l], hy_f_b1[l], hy_f_w2[l], hy_f_b2[l], hy_f_w3[l], hy_f_b3[l],
              hy_f_w4[l], hy_f_freq[l], hy_bias[l])

        x2 = _ffn_sublayer(x2, m4, 0, lat_batch, *ffn1, ln_g[l, 0], ln_b[l, 0], tm)
        ctx2 = _ffn_sublayer(ctx2, m4, 0, ctx_batch, *ffn1, ln_g[l, 0], ln_b[l, 0], tm_ctx)

        qa, ka, va, pb, qc, kc, vc = _mix_in(x2, m4, mix_batch, wts, rope_tabs, mix_tiles_per_seq, tm_mix)
        qa_c, ka_c, va_c, pb_c, qc_c, kc_c, vc_c = _mix_in(ctx2, m4, ctx_batch, wts, None, 1, tm_ctx)
        r3 = lambda a, n: a.reshape(b, n, a.shape[-1])
        ka_c, va_c, kc_c, vc_c = (r3(a, n_ctx) for a in (ka_c, va_c, kc_c, vc_c))

        oa = _gqa_attention(r3(qa, seq), [(ka_c, va_c), (r3(ka, seq), r3(va, seq))], *ATTN_TILES_A)
        oc = _mla_attention(r3(qc, seq), [(kc_c, vc_c), (r3(kc, seq), r3(vc, seq))], *ATTN_TILES_C)
        hy_parts = _hyena(r3(pb, seq), hy, seq)
        flat = lambda a: a.reshape(-1, a.shape[-1])
        out_w = (wts["wa"], wts["wb"], wts["wc"], ln_g[l, 1], ln_b[l, 1])
        if need_ctx:
            oa_c = _gqa_attention(r3(qa_c, n_ctx), [(ka_c, va_c)], *ATTN_TILES_A)
            oc_c = _mla_attention(r3(qc_c, n_ctx), [(kc_c, vc_c)], *ATTN_TILES_C)
            hy_parts_c = _hyena(r3(pb_c, n_ctx), hy, n_ctx)
            ctx2 = _ffn_sublayer(ctx2, m4, 2, ctx_batch, *ffn2, ln_g[l, 2], ln_b[l, 2], tm_ctx,
                                 mix=(flat(oa_c), hy_parts_c, flat(oc_c), *out_w))
        x2 = _ffn_sublayer(x2, m4, 2, lat_batch, *ffn2, ln_g[l, 2], ln_b[l, 2], tm,
                           mix=(flat(oa), hy_parts, flat(oc), *out_w))
    return x2.reshape(b, seq, d)
```
